```python
import math
import jax, jax.numpy as jnp
from jax import lax
import numpy as np

D_MODEL = 1024
BATCH = 8
SEQ = 4096
DEPTH = 4

N_MIXERS = 2
N_ATTN_LAYERS = (DEPTH + 1) // 2
N_CONV_LAYERS = DEPTH // 2

N_HEADS = 16
QK_NOPE_DIM = 64
QK_ROPE_DIM = 32
V_HEAD_DIM = 64
Q_LORA_RANK = 384
KV_LORA_RANK = 256
ROPE_THETA = 10000.0
Q_BLOCK = 128

CONV_WIDTH = 31
CONV_PAD = CONV_WIDTH // 2

FF_MULT = 256
D_FF = ((8 * D_MODEL + 3 * FF_MULT - 1) // (3 * FF_MULT)) * FF_MULT

DEEPNORM_ALPHA = float((2 * DEPTH) ** 0.25)
DEEPNORM_BETA = float((8 * DEPTH) ** -0.25)

LN_EPS = 1e-5
RMS_EPS = 1e-6

kernel_name = "hybrid_mla_conformer_deepnorm_encoder"


def layer_norm(x, g, b):
    xf = x.astype(jnp.float32)
    mu = jnp.mean(xf, axis=-1, keepdims=True)
    var = jnp.mean(jnp.square(xf - mu), axis=-1, keepdims=True)
    y = (xf - mu) * lax.rsqrt(var + LN_EPS) * g.astype(jnp.float32) + b.astype(jnp.float32)
    return y.astype(x.dtype)


def rms_norm(x, g):
    xf = x.astype(jnp.float32)
    y = xf * lax.rsqrt(jnp.mean(jnp.square(xf), axis=-1, keepdims=True) + RMS_EPS)
    return (y * g.astype(jnp.float32)).astype(x.dtype)


def rope_tables(positions, dtype):
    inv_freq = ROPE_THETA ** (-jnp.arange(0, QK_ROPE_DIM, 2, dtype=jnp.float32) / QK_ROPE_DIM)
    ang = positions.astype(jnp.float32)[..., None] * inv_freq
    return jnp.cos(ang).astype(dtype), jnp.sin(ang).astype(dtype)


def apply_rope(t, cos, sin):
    h = QK_ROPE_DIM // 2
    t1, t2 = t[..., :h], t[..., h:]
    return jnp.concatenate([t1 * cos - t2 * sin, t2 * cos + t1 * sin], axis=-1)


def mla_mixer(x, cos, sin, w_dq, q_norm, w_uq, w_dkv, kv_norm, w_ukv, w_o):
    B, S, _ = x.shape
    H = N_HEADS
    cq = rms_norm(x @ w_dq, q_norm)
    q = (cq @ w_uq).reshape(B, S, H, QK_NOPE_DIM + QK_ROPE_DIM)
    q_nope = q[..., :QK_NOPE_DIM]
    q_rope = apply_rope(q[..., QK_NOPE_DIM:], cos[:, :, None, :], sin[:, :, None, :])
    ckv_full = x @ w_dkv
    ckv = rms_norm(ckv_full[..., :KV_LORA_RANK], kv_norm)
    k_rope = apply_rope(ckv_full[..., KV_LORA_RANK:], cos, sin)
    kv = (ckv @ w_ukv).reshape(B, S, H, QK_NOPE_DIM + V_HEAD_DIM)
    k_nope = kv[..., :QK_NOPE_DIM]
    v = kv[..., QK_NOPE_DIM:]
    scale = 1.0 / math.sqrt(QK_NOPE_DIM + QK_ROPE_DIM)

    nblk = S // Q_BLOCK
    qn_blocks = q_nope.reshape(B, nblk, Q_BLOCK, H, QK_NOPE_DIM).transpose(1, 0, 2, 3, 4)
    qr_blocks = q_rope.reshape(B, nblk, Q_BLOCK, H, QK_ROPE_DIM).transpose(1, 0, 2, 3, 4)

    def attend(blk):
        qn, qr = blk
        s = (jnp.einsum('bqhd,bkhd->bhqk', qn, k_nope).astype(jnp.float32)
             + jnp.einsum('bqhd,bkd->bhqk', qr, k_rope).astype(jnp.float32)) * scale
        p = jax.nn.softmax(s, axis=-1).astype(v.dtype)
        return jnp.einsum('bhqk,bkhd->bqhd', p, v)

    o = lax.map(attend, (qn_blocks, qr_blocks))
    o = o.transpose(1, 0, 2, 3, 4).reshape(B, S, H * V_HEAD_DIM)
    return o @ w_o


def conformer_conv_mixer(x, w_in, b_in, dw, dw_b, ln_g, ln_b, w_out, b_out):
    D = x.shape[-1]
    h = x @ w_in + b_in
    h = h[..., :D] * jax.nn.sigmoid(h[..., D:])
    h = lax.conv_general_dilated(
        h, dw[:, None, :].astype(h.dtype), window_strides=(1,),
        padding=[(CONV_PAD, CONV_PAD)],
        dimension_numbers=('NWC', 'WIO', 'NWC'),
        feature_group_count=D) + dw_b
    h = layer_norm(h, ln_g, ln_b)
    h = jax.nn.silu(h)
    return h @ w_out + b_out


def swiglu_ffn(x, w1, w3, w2):
    return (jax.nn.silu(x @ w1) * (x @ w3)) @ w2


def setup_inputs(seed: int = 0) -> dict:
    key = jax.random.key(seed)
    ks = jax.random.split(key, 24)
    f32 = jnp.float32
    D, H = D_MODEL, N_HEADS
    La, Lc = N_ATTN_LAYERS, N_CONV_LAYERS

    def w(k, shape, fan_in, mult=1.0):
        return jax.random.normal(k, shape, f32) * (fan_in ** -0.5) * mult

    x = jax.random.normal(ks[0], (BATCH, SEQ, D), f32)
    offsets = jax.random.randint(ks[1], (BATCH, 1), 0, 1024, dtype=jnp.int32)
    positions = (jnp.arange(SEQ, dtype=jnp.int32)[None, :] + offsets).astype(jnp.int32)

    ln_g = 1.0 + 0.02 * jax.random.normal(ks[2], (DEPTH, 2, D), f32)
    ln_b = 0.02 * jax.random.normal(ks[3], (DEPTH, 2, D), f32)

    mla_w_dq = w(ks[4], (La, D, Q_LORA_RANK), D)
    mla_q_norm = 1.0 + 0.02 * jax.random.normal(ks[5], (La, Q_LORA_RANK), f32)
    mla_w_uq = w(ks[6], (La, Q_LORA_RANK, H * (QK_NOPE_DIM + QK_ROPE_DIM)), Q_LORA_RANK)
    mla_w_dkv = w(ks[7], (La, D, KV_LORA_RANK + QK_ROPE_DIM), D)
    mla_kv_norm = 1.0 + 0.02 * jax.random.normal(ks[8], (La, KV_LORA_RANK), f32)
    mla_w_ukv = w(ks[9], (La, KV_LORA_RANK, H * (QK_NOPE_DIM + V_HEAD_DIM)), KV_LORA_RANK)
    mla_w_o = w(ks[10], (La, H * V_HEAD_DIM, D), H * V_HEAD_DIM, DEEPNORM_BETA)

    conv_w_in = w(ks[11], (Lc, D, 2 * D), D)
    conv_b_in = 0.02 * jax.random.normal(ks[12], (Lc, 2 * D), f32)
    conv_dw = w(ks[13], (Lc, CONV_WIDTH, D), CONV_WIDTH)
    conv_dw_b = 0.02 * jax.random.normal(ks[14], (Lc, D), f32)
    conv_ln_g = 1.0 + 0.02 * jax.random.normal(ks[15], (Lc, D), f32)
    conv_ln_b = 0.02 * jax.random.normal(ks[16], (Lc, D), f32)
    conv_w_out = w(ks[17], (Lc, D, D), D, DEEPNORM_BETA)
    conv_b_out = 0.02 * jax.random.normal(ks[18], (Lc, D), f32)

    ffn_w1 = w(ks[19], (DEPTH, D, D_FF), D)
    ffn_w3 = w(ks[20], (DEPTH, D, D_FF), D)
    ffn_w2 = w(ks[21], (DEPTH, D_FF, D), D_FF, DEEPNORM_BETA)

    return {
        "x": x, "positions": positions, "ln_g": ln_g, "ln_b": ln_b,
        "mla_w_dq": mla_w_dq, "mla_q_norm": mla_q_norm, "mla_w_uq": mla_w_uq,
        "mla_w_dkv": mla_w_dkv, "mla_kv_norm": mla_kv_norm, "mla_w_ukv": mla_w_ukv,
        "mla_w_o": mla_w_o,
        "conv_w_in": conv_w_in, "conv_b_in": conv_b_in, "conv_dw": conv_dw,
        "conv_dw_b": conv_dw_b, "conv_ln_g": conv_ln_g, "conv_ln_b": conv_ln_b,
        "conv_w_out": conv_w_out, "conv_b_out": conv_b_out,
        "ffn_w1": ffn_w1, "ffn_w3": ffn_w3, "ffn_w2": ffn_w2,
    }


def reference(x, positions, ln_g, ln_b,
              mla_w_dq, mla_q_norm, mla_w_uq, mla_w_dkv, mla_kv_norm, mla_w_ukv, mla_w_o,
              conv_w_in, conv_b_in, conv_dw, conv_dw_b, conv_ln_g, conv_ln_b,
              conv_w_out, conv_b_out,
              ffn_w1, ffn_w3, ffn_w2):
    cos, sin = rope_tables(positions, x.dtype)
    for i in range(DEPTH):
        j = i // N_MIXERS
        if i % N_MIXERS == 0:
            h = mla_mixer(x, cos, sin, mla_w_dq[j], mla_q_norm[j], mla_w_uq[j],
                          mla_w_dkv[j], mla_kv_norm[j], mla_w_ukv[j], mla_w_o[j])
        else:
            h = conformer_conv_mixer(x, conv_w_in[j], conv_b_in[j], conv_dw[j], conv_dw_b[j],
                                     conv_ln_g[j], conv_ln_b[j], conv_w_out[j], conv_b_out[j])
        x = layer_norm(DEEPNORM_ALPHA * x + h, ln_g[i, 0], ln_b[i, 0])
        f = swiglu_ffn(x, ffn_w1[i], ffn_w3[i], ffn_w2[i])
        x = layer_norm(DEEPNORM_ALPHA * x + f, ln_g[i, 1], ln_b[i, 1])
    return x
```

```python
import functools
import math

import jax
import jax.numpy as jnp
from jax import lax
from jax.experimental import pallas as pl
from jax.experimental.pallas import tpu as pltpu

F32 = jnp.float32
BF16 = jnp.bfloat16

N_HEADS = 16
QK_NOPE_DIM = 64
QK_ROPE_DIM = 32
V_HEAD_DIM = 64
ROPE_THETA = 10000.0
LN_EPS = 1e-5
RMS_EPS = 1e-6

V7X_LANES = 128
V7X_VMEM_BYTES = 64 * 1024 * 1024
VMEM_LIMIT = 48 * 1024 * 1024

HALO = 16


def _cparams(n_axes):
    return pltpu.CompilerParams(
        dimension_semantics=("arbitrary",) * n_axes, vmem_limit_bytes=VMEM_LIMIT)


def _const_spec(shape):
    nd = len(shape)
    return pl.BlockSpec(shape, lambda *_: (0,) * nd)


def _layer_norm(z, g, b):
    mu = jnp.mean(z, axis=-1, keepdims=True)
    d = z - mu
    var = jnp.mean(d * d, axis=-1, keepdims=True)
    return d * lax.rsqrt(var + LN_EPS) * g + b


def _sigmoid(a):
    return 1.0 / (1.0 + jnp.exp(-a))


def _rope_kernel(pos_ref, inv_ref, cos_ref, sin_ref):
    ang = inv_ref[...] * pos_ref[0].astype(F32)
    cos_ref[0] = jnp.cos(ang)
    sin_ref[0] = jnp.sin(ang)


def _rope_tables(positions):
    B, S = positions.shape
    hr = QK_ROPE_DIM // 2
    inv_freq = ROPE_THETA ** (-jnp.arange(0, QK_ROPE_DIM, 2, dtype=F32) / QK_ROPE_DIM)
    return pl.pallas_call(
        _rope_kernel,
        grid=(B,),
        in_specs=[pl.BlockSpec((1, 1, S), lambda b: (b, 0, 0)), _const_spec((hr, 1))],
        out_specs=[pl.BlockSpec((1, hr, S), lambda b: (b, 0, 0))] * 2,
        out_shape=[jax.ShapeDtypeStruct((B, hr, S), F32)] * 2,
        compiler_params=_cparams(1),
        name="rope_tables",
    )(positions.reshape(B, 1, S), inv_freq.reshape(hr, 1))


def _mla_proj_kernel(x_ref, cos_ref, sin_ref, wcat_ref, qn_ref, kvn_ref, wuq_ref, wuv_ref, wuk_ref,
                     qT_ref, k_ref, vT_ref, *, ql, kvl, qscale):
    H, dn, dr, dv = N_HEADS, QK_NOPE_DIM, QK_ROPE_DIM, V_HEAD_DIM
    hr = dr // 2
    tp = x_ref.shape[1]
    xb = x_ref[0].astype(BF16)
    zT = lax.dot_general(wcat_ref[...], xb, (((1,), (1,)), ((), ())),
                         preferred_element_type=F32)
    cqT = zT[0:ql]
    ckvT = zT[ql:ql + kvl]
    krT = zT[ql + kvl:ql + kvl + dr]
    cqT = cqT * lax.rsqrt(jnp.mean(cqT * cqT, axis=0, keepdims=True) + RMS_EPS) * qn_ref[...]
    ckvT = ckvT * lax.rsqrt(jnp.mean(ckvT * ckvT, axis=0, keepdims=True) + RMS_EPS) * kvn_ref[...]
    cos = cos_ref[0]
    sin = sin_ref[0]

    qT = jnp.dot(wuq_ref[...], cqT.astype(BF16), preferred_element_type=F32)
    qT = qT.reshape(H, dn + dr, tp)
    t1 = qT[:, dn:dn + hr]
    t2 = qT[:, dn + hr:dn + dr]
    qT_ref[0, :, 0:dn] = (qT[:, 0:dn] * qscale).astype(BF16)
    qT_ref[0, :, dn:dn + hr] = ((t1 * cos - t2 * sin) * qscale).astype(BF16)
    qT_ref[0, :, dn + hr:dn + dr] = ((t2 * cos + t1 * sin) * qscale).astype(BF16)
    qT_ref[0, :, dn + dr:] = jnp.zeros((H, qT_ref.shape[2] - dn - dr, tp), BF16)

    ckvb = ckvT.astype(BF16)
    vT = jnp.dot(wuv_ref[...], ckvb, preferred_element_type=F32)
    vT_ref[0] = vT.reshape(H, dv, tp).astype(BF16)

    kw = k_ref.shape[3]
    kpad = jnp.dot(ckvT.T.astype(BF16), wuk_ref[...], preferred_element_type=F32)
    k1 = krT[0:hr]
    k2 = krT[hr:dr]
    krT_pad = jnp.concatenate(
        [jnp.zeros((dn, tp), F32), k1 * cos - k2 * sin, k2 * cos + k1 * sin,
         jnp.zeros((kw - dn - dr, tp), F32)], axis=0)
    kr_pad = krT_pad.T
    for h in range(H):
        k_ref[0, h] = (kpad[:, h * kw:(h + 1) * kw] + kr_pad).astype(BF16)


def _mla_proj(x, cos, sin, w_dq, q_norm, w_uq, w_dkv, kv_norm, w_ukv, *, tp=512):
    B, S, D = x.shape
    H, dn, dr, dv = N_HEADS, QK_NOPE_DIM, QK_ROPE_DIM, V_HEAD_DIM
    hr = dr // 2
    ql = w_dq.shape[1]
    kvl = kv_norm.shape[0]
    kw = V7X_LANES
    wcatT = jnp.concatenate([w_dq, w_dkv], axis=1).T.astype(BF16)
    wuqT = w_uq.T.astype(BF16)
    w_ukv3 = w_ukv.reshape(kvl, H, dn + dv)
    wuvT = w_ukv3[:, :, dn:].reshape(kvl, H * dv).T.astype(BF16)
    wukp = jnp.pad(w_ukv3[:, :, :dn], ((0, 0), (0, 0), (0, kw - dn))).reshape(kvl, H * kw).astype(BF16)
    qscale = math.log2(math.e) / math.sqrt(dn + dr)
    kern = functools.partial(_mla_proj_kernel, ql=ql, kvl=kvl, qscale=qscale)
    return pl.pallas_call(
        kern,
        grid=(B, S // tp),
        in_specs=[
            pl.BlockSpec((1, tp, D), lambda b, i: (b, i, 0)),
            pl.BlockSpec((1, hr, tp), lambda b, i: (b, 0, i)),
            pl.BlockSpec((1, hr, tp), lambda b, i: (b, 0, i)),
            _const_spec(wcatT.shape), _const_spec((ql, 1)), _const_spec((kvl, 1)),
            _const_spec(wuqT.shape), _const_spec(wuvT.shape), _const_spec(wukp.shape),
        ],
        out_specs=[
            pl.BlockSpec((1, H, kw, tp), lambda b, i: (b, 0, 0, i)),
            pl.BlockSpec((1, H, tp, kw), lambda b, i: (b, 0, i, 0)),
            pl.BlockSpec((1, H, dv, tp), lambda b, i: (b, 0, 0, i)),
        ],
        out_shape=[
            jax.ShapeDtypeStruct((B, H, kw, S), BF16),
            jax.ShapeDtypeStruct((B, H, S, kw), BF16),
            jax.ShapeDtypeStruct((B, H, dv, S), BF16),
        ],
        compiler_params=_cparams(2),
        name="mla_proj",
    )(x, cos, sin, wcatT, q_norm.reshape(ql, 1), kv_norm.reshape(kvl, 1), wuqT, wuvT, wukp)


def _attn_kernel(qT_ref, k_ref, vT_ref, o_ref, *, tk):
    hb = qT_ref.shape[1]
    tq = qT_ref.shape[3]
    S = k_ref.shape[2]
    dv = vT_ref.shape[2]
    outs = []
    for hh in range(hb):
        qT = qT_ref[0, hh]
        m = jnp.full((1, tq), -jnp.inf, F32)
        l = jnp.zeros((1, tq), F32)
        acc = jnp.zeros((dv, tq), F32)
        for c in range(S // tk):
            kc = k_ref[0, hh, c * tk:(c + 1) * tk, :]
            s = jnp.dot(kc, qT, preferred_element_type=F32)
            m_new = jnp.maximum(m, jnp.max(s, axis=0, keepdims=True))
            alpha = jnp.exp2(m - m_new)
            p = jnp.exp2(s - m_new)
            l = alpha * l + jnp.sum(p, axis=0, keepdims=True)
            vc = vT_ref[0, hh, :, c * tk:(c + 1) * tk]
            acc = alpha * acc + jnp.dot(vc, p.astype(BF16), preferred_element_type=F32)
            m = m_new
        outs.append(acc / l)
    oT = jnp.concatenate(outs, axis=0)
    o_ref[0] = oT.T.astype(o_ref.dtype)


def _attention(qT, k, vT, *, tq=256, tk=512, hb=2):
    B, H, kw, S = qT.shape
    dv = vT.shape[2]
    return pl.pallas_call(
        functools.partial(_attn_kernel, tk=tk),
        grid=(B, H // hb, S // tq),
        in_specs=[
            pl.BlockSpec((1, hb, kw, tq), lambda b, h, i: (b, h, 0, i)),
            pl.BlockSpec((1, hb, S, kw), lambda b, h, i: (b, h, 0, 0)),
            pl.BlockSpec((1, hb, dv, S), lambda b, h, i: (b, h, 0, 0)),
        ],
        out_specs=pl.BlockSpec((1, tq, hb * dv), lambda b, h, i: (b, i, h)),
        out_shape=jax.ShapeDtypeStruct((B, S, H * dv), BF16),
        compiler_params=_cparams(3),
        name="mla_attention",
    )(qT, k, vT)


def _proj_ln_kernel(x_ref, a_ref, w_ref, g_ref, b_ref, o_ref, *, alpha):
    h = jnp.dot(a_ref[...], w_ref[...], preferred_element_type=F32)
    o_ref[...] = _layer_norm(alpha * x_ref[...] + h, g_ref[...], b_ref[...])


def _proj_ln(x2, a2, w, g, b, alpha, *, tm=512):
    T, D = x2.shape
    K = a2.shape[1]
    return pl.pallas_call(
        functools.partial(_proj_ln_kernel, alpha=alpha),
        grid=(T // tm,),
        in_specs=[
            pl.BlockSpec((tm, D), lambda i: (i, 0)),
            pl.BlockSpec((tm, K), lambda i: (i, 0)),
            _const_spec((K, D)), _const_spec((1, D)), _const_spec((1, D)),
        ],
        out_specs=pl.BlockSpec((tm, D), lambda i: (i, 0)),
        out_shape=jax.ShapeDtypeStruct((T, D), F32),
        compiler_params=_cparams(1),
        name="out_proj_ln",
    )(x2, a2, w.astype(BF16), g.reshape(1, D), b.reshape(1, D))


def _ffn_kernel(x_ref, w1_ref, w3_ref, w2_ref, g_ref, b_ref, o_ref, gate_ref, *, alpha, fc):
    x = x_ref[...]
    xb = x.astype(BF16)
    dff = w1_ref.shape[1]
    for c in range(dff // fc):
        a = jnp.dot(xb, w1_ref[:, c * fc:(c + 1) * fc], preferred_element_type=F32)
        u = jnp.dot(xb, w3_ref[:, c * fc:(c + 1) * fc], preferred_element_type=F32)
        gate_ref[:, c * fc:(c + 1) * fc] = (a * _sigmoid(a) * u).astype(BF16)
    f = jnp.dot(gate_ref[...], w2_ref[...], preferred_element_type=F32)
    o_ref[...] = _layer_norm(alpha * x + f, g_ref[...], b_ref[...])


def _ffn_ln(x2, w1, w3, w2, g, b, alpha, *, tm=512, fc=256):
    T, D = x2.shape
    dff = w1.shape[1]
    return pl.pallas_call(
        functools.partial(_ffn_kernel, alpha=alpha, fc=fc),
        grid=(T // tm,),
        in_specs=[
            pl.BlockSpec((tm, D), lambda i: (i, 0)),
            _const_spec((D, dff)), _const_spec((D, dff)), _const_spec((dff, D)),
            _const_spec((1, D)), _const_spec((1, D)),
        ],
        out_specs=pl.BlockSpec((tm, D), lambda i: (i, 0)),
        out_shape=jax.ShapeDtypeStruct((T, D), F32),
        scratch_shapes=[pltpu.VMEM((tm, dff), BF16)],
        compiler_params=_cparams(1),
        name="ffn_ln",
    )(x2, w1.astype(BF16), w3.astype(BF16), w2.astype(BF16), g.reshape(1, D), b.reshape(1, D))


def _glu_kernel(x_ref, w_ref, b_ref, o_ref):
    D = o_ref.shape[1]
    h = jnp.dot(x_ref[...].astype(BF16), w_ref[...], preferred_element_type=F32) + b_ref[...]
    o_ref[...] = h[:, :D] * _sigmoid(h[:, D:])


def _conv_glu(x2, w_in, b_in, *, tm=512):
    T, D = x2.shape
    return pl.pallas_call(
        _glu_kernel,
        grid=(T // tm,),
        in_specs=[pl.BlockSpec((tm, D), lambda i: (i, 0)), _const_spec((D, 2 * D)), _const_spec((1, 2 * D))],
        out_specs=pl.BlockSpec((tm, D), lambda i: (i, 0)),
        out_shape=jax.ShapeDtypeStruct((T, D), F32),
        compiler_params=_cparams(1),
        name="conv_glu",
    )(x2, w_in.astype(BF16), b_in.reshape(1, 2 * D))


def _conv_mid_kernel(x_ref, h_ref, hp_ref, hn_ref, dw_ref, dwb_ref, lg_ref, lb_ref, wo_ref, bo_ref,
                     g_ref, b_ref, o_ref, buf_ref, cv_ref, *, alpha, width, rb):
    ts = h_ref.shape[1]
    nl = buf_ref.shape[0]
    L = V7X_LANES
    i = pl.program_id(1)
    n = pl.num_programs(1)
    prev_ok = (i > 0).astype(F32)
    next_ok = (i < n - 1).astype(F32)
    for lc in range(nl):
        buf_ref[lc, 0:HALO, :] = hp_ref[0, :, lc * L:(lc + 1) * L] * prev_ok
        buf_ref[lc, HALO:HALO + ts, :] = h_ref[0, :, lc * L:(lc + 1) * L]
        buf_ref[lc, HALO + ts:, :] = hn_ref[0, :, lc * L:(lc + 1) * L] * next_ok

    pad = width // 2
    off = HALO - pad

    def lane_chunk(lc, carry):
        for r0 in range(0, ts, rb):
            win = buf_ref[lc, r0:r0 + rb + 2 * HALO, :]
            acc = jnp.zeros((rb, L), F32)
            for s in range(8):
                sh = win[s:s + rb + 2 * HALO - 8]
                for a in range((2 * HALO) // 8):
                    j = 8 * a + s - off
                    if 0 <= j < width:
                        acc = acc + sh[8 * a:8 * a + rb] * dw_ref[lc, j:j + 1, :]
            cv_ref[lc, r0:r0 + rb, :] = acc + dwb_ref[lc]
        return carry

    lax.fori_loop(0, nl, lane_chunk, 0)

    cv = jnp.concatenate([cv_ref[lc] for lc in range(nl)], axis=1)
    y = _layer_norm(cv, lg_ref[...], lb_ref[...])
    y = y * _sigmoid(y)
    hout = jnp.dot(y.astype(BF16), wo_ref[...], preferred_element_type=F32) + bo_ref[...]
    o_ref[0] = _layer_norm(alpha * x_ref[0] + hout, g_ref[...], b_ref[...])


def _conv_mid(x, h, dw, dw_b, ln_g, ln_b, w_out, b_out, g, b, alpha, *, ts=512, rb=256):
    B, S, D = x.shape
    width = dw.shape[0]
    L = V7X_LANES
    nl = D // L
    assert width // 2 <= HALO and ts % HALO == 0 and ts % rb == 0
    hblk = ts // HALO
    dw3 = jnp.pad(dw, ((0, 2 * HALO - width), (0, 0))).reshape(2 * HALO, nl, L).transpose(1, 0, 2)
    dwb3 = dw_b.reshape(nl, 1, L)
    row = lambda v: v.reshape(1, D)
    kern = functools.partial(_conv_mid_kernel, alpha=alpha, width=width, rb=rb)
    return pl.pallas_call(
        kern,
        grid=(B, S // ts),
        in_specs=[
            pl.BlockSpec((1, ts, D), lambda bb, i: (bb, i, 0)),
            pl.BlockSpec((1, ts, D), lambda bb, i: (bb, i, 0)),
            pl.BlockSpec((1, HALO, D), lambda bb, i: (bb, jnp.maximum(i * hblk - 1, 0), 0)),
            pl.BlockSpec((1, HALO, D), lambda bb, i: (bb, jnp.minimum((i + 1) * hblk, S // HALO - 1), 0)),
            _const_spec(dw3.shape), _const_spec(dwb3.shape),
            _const_spec((1, D)), _const_spec((1, D)),
            _const_spec((D, D)), _const_spec((1, D)),
            _const_spec((1, D)), _const_spec((1, D)),
        ],
        out_specs=pl.BlockSpec((1, ts, D), lambda bb, i: (bb, i, 0)),
        out_shape=jax.ShapeDtypeStruct((B, S, D), F32),
        scratch_shapes=[pltpu.VMEM((nl, ts + 2 * HALO, L), F32), pltpu.VMEM((nl, ts, L), F32)],
        compiler_params=_cparams(2),
        name="conv_mid",
    )(x, h, h, h, dw3, dwb3, row(ln_g), row(ln_b), w_out.astype(BF16), row(b_out), row(g), row(b))


def kernel(x, positions, ln_g, ln_b, mla_w_dq, mla_q_norm, mla_w_uq, mla_w_dkv, mla_kv_norm, mla_w_ukv, mla_w_o, conv_w_in, conv_b_in, conv_dw, conv_dw_b, conv_ln_g, conv_ln_b, conv_w_out, conv_b_out, ffn_w1, ffn_w3, ffn_w2):
    B, S, D = x.shape
    depth = ln_g.shape[0]
    alpha = float((2 * depth) ** 0.25)
    cos, sin = _rope_tables(positions)
    for i in range(depth):
        j = i // 2
        if i % 2 == 0:
            qT, k, vT = _mla_proj(x, cos, sin, mla_w_dq[j], mla_q_norm[j], mla_w_uq[j],
                                  mla_w_dkv[j], mla_kv_norm[j], mla_w_ukv[j])
            o = _attention(qT, k, vT)
            x2 = _proj_ln(x.reshape(B * S, D), o.reshape(B * S, -1), mla_w_o[j],
                          ln_g[i, 0], ln_b[i, 0], alpha)
        else:
            h = _conv_glu(x.reshape(B * S, D), conv_w_in[j], conv_b_in[j]).reshape(B, S, D)
            x2 = _conv_mid(x, h, conv_dw[j], conv_dw_b[j], conv_ln_g[j], conv_ln_b[j],
                           conv_w_out[j], conv_b_out[j], ln_g[i, 0], ln_b[i, 0], alpha).reshape(B * S, D)
        x2 = _ffn_ln(x2, ffn_w1[i], ffn_w3[i], ffn_w2[i], ln_g[i, 1], ln_b[i, 1], alpha)
        x = x2.reshape(B, S, D)
    return x
```

```python
import functools
import math

import jax
import jax.numpy as jnp
from jax import lax
from jax.experimental import pallas as pl
from jax.experimental.pallas import tpu as pltpu

F32 = jnp.float32
BF16 = jnp.bfloat16

N_HEADS = 16
QK_NOPE_DIM = 64
QK_ROPE_DIM = 32
V_HEAD_DIM = 64
ROPE_THETA = 10000.0
LN_EPS = 1e-5
RMS_EPS = 1e-6

V7X_LANES = 128
V7X_VMEM_BYTES = 64 * 1024 * 1024
VMEM_LIMIT = 48 * 1024 * 1024

HALO = 16


def _cparams(n_axes):
    return pltpu.CompilerParams(
        dimension_semantics=("arbitrary",) * n_axes, vmem_limit_bytes=VMEM_LIMIT)


def _const_spec(shape):
    nd = len(shape)
    return pl.BlockSpec(shape, lambda *_: (0,) * nd)


def _layer_norm(z, g, b):
    mu = jnp.mean(z, axis=-1, keepdims=True)
    d = z - mu
    var = jnp.mean(d * d, axis=-1, keepdims=True)
    return d * lax.rsqrt(var + LN_EPS) * g + b


def _sigmoid(a):
    return 1.0 / (1.0 + jnp.exp(-a))


def _rope_kernel(pos_ref, inv_ref, cos_ref, sin_ref):
    ang = inv_ref[...] * pos_ref[0].astype(F32)
    cos_ref[0] = jnp.cos(ang)
    sin_ref[0] = jnp.sin(ang)


def _rope_tables(positions):
    B, S = positions.shape
    hr = QK_ROPE_DIM // 2
    inv_freq = ROPE_THETA ** (-jnp.arange(0, QK_ROPE_DIM, 2, dtype=F32) / QK_ROPE_DIM)
    return pl.pallas_call(
        _rope_kernel,
        grid=(B,),
        in_specs=[pl.BlockSpec((1, 1, S), lambda b: (b, 0, 0)), _const_spec((hr, 1))],
        out_specs=[pl.BlockSpec((1, hr, S), lambda b: (b, 0, 0))] * 2,
        out_shape=[jax.ShapeDtypeStruct((B, hr, S), F32)] * 2,
        compiler_params=_cparams(1),
        name="rope_tables",
    )(positions.reshape(B, 1, S), inv_freq.reshape(hr, 1))


def _mla_proj_kernel(x_ref, cos_ref, sin_ref, wcat_ref, qn_ref, kvn_ref, wuq_ref, wuv_ref, wuk_ref,
                     qT_ref, k_ref, vT_ref, *, ql, kvl, qscale):
    H, dn, dr, dv = N_HEADS, QK_NOPE_DIM, QK_ROPE_DIM, V_HEAD_DIM
    hr = dr // 2
    tp = x_ref.shape[1]
    xb = x_ref[0].astype(BF16)
    zT = lax.dot_general(wcat_ref[...], xb, (((1,), (1,)), ((), ())),
                         preferred_element_type=F32)
    cqT = zT[0:ql]
    ckvT = zT[ql:ql + kvl]
    krT = zT[ql + kvl:ql + kvl + dr]
    cqT = cqT * lax.rsqrt(jnp.mean(cqT * cqT, axis=0, keepdims=True) + RMS_EPS) * qn_ref[...]
    ckvT = ckvT * lax.rsqrt(jnp.mean(ckvT * ckvT, axis=0, keepdims=True) + RMS_EPS) * kvn_ref[...]
    cos = cos_ref[0]
    sin = sin_ref[0]

    qT = jnp.dot(wuq_ref[...], cqT.astype(BF16), preferred_element_type=F32)
    qT = qT.reshape(H, dn + dr, tp)
    t1 = qT[:, dn:dn + hr]
    t2 = qT[:, dn + hr:dn + dr]
    r1 = (t1 * cos - t2 * sin) * qscale
    r2 = (t2 * cos + t1 * sin) * qscale
    qn = qT[:, 0:dn] * qscale
    nqb, kwq, tq = qT_ref.shape[2:]
    for j in range(nqb):
        cols = slice(j * tq, (j + 1) * tq)
        qT_ref[0, :, j, 0:dn] = qn[:, :, cols].astype(BF16)
        qT_ref[0, :, j, dn:dn + hr] = r1[:, :, cols].astype(BF16)
        qT_ref[0, :, j, dn + hr:dn + dr] = r2[:, :, cols].astype(BF16)
        qT_ref[0, :, j, dn + dr:] = jnp.zeros((H, kwq - dn - dr, tq), BF16)

    ckvb = ckvT.astype(BF16)
    vT = jnp.dot(wuv_ref[...], ckvb, preferred_element_type=F32)
    vT_ref[0] = vT.reshape(H, dv, tp).astype(BF16)

    kw = k_ref.shape[3]
    kpad = jnp.dot(ckvT.T.astype(BF16), wuk_ref[...], preferred_element_type=F32)
    k1 = krT[0:hr]
    k2 = krT[hr:dr]
    krT_pad = jnp.concatenate(
        [jnp.zeros((dn, tp), F32), k1 * cos - k2 * sin, k2 * cos + k1 * sin,
         jnp.zeros((kw - dn - dr, tp), F32)], axis=0)
    kr_pad = krT_pad.T
    for h in range(H):
        k_ref[0, h] = (kpad[:, h * kw:(h + 1) * kw] + kr_pad).astype(BF16)


def _mla_proj(x, cos, sin, w_dq, q_norm, w_uq, w_dkv, kv_norm, w_ukv, *, tq, tp=512):
    B, S, D = x.shape
    H, dn, dr, dv = N_HEADS, QK_NOPE_DIM, QK_ROPE_DIM, V_HEAD_DIM
    hr = dr // 2
    ql = w_dq.shape[1]
    kvl = kv_norm.shape[0]
    kw = V7X_LANES
    wcatT = jnp.concatenate([w_dq, w_dkv], axis=1).T.astype(BF16)
    wuqT = w_uq.T.astype(BF16)
    w_ukv3 = w_ukv.reshape(kvl, H, dn + dv)
    wuvT = w_ukv3[:, :, dn:].reshape(kvl, H * dv).T.astype(BF16)
    wukp = jnp.pad(w_ukv3[:, :, :dn], ((0, 0), (0, 0), (0, kw - dn))).reshape(kvl, H * kw).astype(BF16)
    qscale = math.log2(math.e) / math.sqrt(dn + dr)
    kern = functools.partial(_mla_proj_kernel, ql=ql, kvl=kvl, qscale=qscale)
    return pl.pallas_call(
        kern,
        grid=(B, S // tp),
        in_specs=[
            pl.BlockSpec((1, tp, D), lambda b, i: (b, i, 0)),
            pl.BlockSpec((1, hr, tp), lambda b, i: (b, 0, i)),
            pl.BlockSpec((1, hr, tp), lambda b, i: (b, 0, i)),
            _const_spec(wcatT.shape), _const_spec((ql, 1)), _const_spec((kvl, 1)),
            _const_spec(wuqT.shape), _const_spec(wuvT.shape), _const_spec(wukp.shape),
        ],
        out_specs=[
            pl.BlockSpec((1, H, tp // tq, kw, tq), lambda b, i: (b, 0, i, 0, 0)),
            pl.BlockSpec((1, H, tp, kw), lambda b, i: (b, 0, i, 0)),
            pl.BlockSpec((1, H, dv, tp), lambda b, i: (b, 0, 0, i)),
        ],
        out_shape=[
            jax.ShapeDtypeStruct((B, H, S // tq, kw, tq), BF16),
            jax.ShapeDtypeStruct((B, H, S, kw), BF16),
            jax.ShapeDtypeStruct((B, H, dv, S), BF16),
        ],
        compiler_params=_cparams(2),
        name="mla_proj",
    )(x, cos, sin, wcatT, q_norm.reshape(ql, 1), kv_norm.reshape(kvl, 1), wuqT, wuvT, wukp)


def _attn_kernel(qT_ref, k_ref, vT_ref, o_ref, sa_ref, sb_ref, ma_ref, mb_ref, oT_ref, *, tk):
    hb, nqb, _, tq = qT_ref.shape[1:]
    S = k_ref.shape[2]
    dv = vT_ref.shape[2]
    nc = S // tk
    n_items = hb * nqb

    def stage(t1, s1_ref, m1_ref, t2, s2_ref, m2_ref):
        if t1 is not None:
            h1 = t1 // nqb
            qT = qT_ref[0, h1, t1 % nqb]
            mx = jnp.full((8, tq), -jnp.inf, F32)
        if t2 is not None:
            h2 = t2 // nqb
            m8 = jnp.broadcast_to(jnp.max(m2_ref[...], axis=0, keepdims=True), (8, tq))
            l8 = jnp.zeros((8, tq), F32)
            acc = jnp.zeros((dv, tq), F32)
        for c in range(nc):
            rows = slice(c * tk, (c + 1) * tk)
            if t1 is not None:
                s = jnp.dot(k_ref[0, h1, rows, :], qT, preferred_element_type=F32)
                s1_ref[rows, :] = s
                mx = jnp.maximum(mx, jnp.max(s.reshape(tk // 8, 8, tq), axis=0))
            if t2 is not None:
                p = jnp.exp2(s2_ref[rows, :].reshape(tk // 8, 8, tq) - m8)
                l8 = l8 + jnp.sum(p, axis=0)
                acc = acc + jnp.dot(vT_ref[0, h2, :, rows], p.reshape(tk, tq).astype(BF16),
                                    preferred_element_type=F32)
        if t1 is not None:
            m1_ref[...] = mx
        if t2 is not None:
            oT_ref[h2, t2 % nqb] = acc / jnp.sum(l8, axis=0, keepdims=True)

    stage(0, sa_ref, ma_ref, None, None, None)

    def pair(j, carry):
        stage(2 * j + 1, sb_ref, mb_ref, 2 * j, sa_ref, ma_ref)
        stage(2 * j + 2, sa_ref, ma_ref, 2 * j + 1, sb_ref, mb_ref)
        return carry

    lax.fori_loop(0, n_items // 2 - 1, pair, 0)
    stage(n_items - 1, sb_ref, mb_ref, n_items - 2, sa_ref, ma_ref)
    stage(None, None, None, n_items - 1, sb_ref, mb_ref)

    for qb in range(nqb):
        oT = jnp.concatenate([oT_ref[h, qb] for h in range(hb)], axis=0)
        o_ref[0, qb * tq:(qb + 1) * tq, :] = oT.T.astype(o_ref.dtype)


def _attention(qT, k, vT, *, tk=512, hb=2):
    B, H, nqb, kw, tq = qT.shape
    S = k.shape[2]
    dv = vT.shape[2]
    assert (hb * nqb) % 2 == 0 and hb * dv == V7X_LANES
    return pl.pallas_call(
        functools.partial(_attn_kernel, tk=tk),
        grid=(B, H // hb),
        in_specs=[
            pl.BlockSpec((1, hb, nqb, kw, tq), lambda b, h: (b, h, 0, 0, 0)),
            pl.BlockSpec((1, hb, S, kw), lambda b, h: (b, h, 0, 0)),
            pl.BlockSpec((1, hb, dv, S), lambda b, h: (b, h, 0, 0)),
        ],
        out_specs=pl.BlockSpec((1, S, hb * dv), lambda b, h: (b, 0, h)),
        out_shape=jax.ShapeDtypeStruct((B, S, H * dv), BF16),
        scratch_shapes=[
            pltpu.VMEM((S, tq), F32), pltpu.VMEM((S, tq), F32),
            pltpu.VMEM((8, tq), F32), pltpu.VMEM((8, tq), F32),
            pltpu.VMEM((hb, nqb, dv, tq), F32),
        ],
        compiler_params=_cparams(2),
        name="mla_attention",
    )(qT, k, vT)


def _proj_ln_kernel(x_ref, a_ref, w_ref, g_ref, b_ref, o_ref, *, alpha):
    h = jnp.dot(a_ref[...], w_ref[...], preferred_element_type=F32)
    o_ref[...] = _layer_norm(alpha * x_ref[...] + h, g_ref[...], b_ref[...])


def _proj_ln(x2, a2, w, g, b, alpha, *, tm=512):
    T, D = x2.shape
    K = a2.shape[1]
    return pl.pallas_call(
        functools.partial(_proj_ln_kernel, alpha=alpha),
        grid=(T // tm,),
        in_specs=[
            pl.BlockSpec((tm, D), lambda i: (i, 0)),
            pl.BlockSpec((tm, K), lambda i: (i, 0)),
            _const_spec((K, D)), _const_spec((1, D)), _const_spec((1, D)),
        ],
        out_specs=pl.BlockSpec((tm, D), lambda i: (i, 0)),
        out_shape=jax.ShapeDtypeStruct((T, D), F32),
        compiler_params=_cparams(1),
        name="out_proj_ln",
    )(x2, a2, w.astype(BF16), g.reshape(1, D), b.reshape(1, D))


def _ffn_kernel(x_ref, w1_ref, w3_ref, w2_ref, g_ref, b_ref, o_ref, gate_ref, *, alpha, fc):
    x = x_ref[...]
    xb = x.astype(BF16)
    dff = w1_ref.shape[1]
    for c in range(dff // fc):
        a = jnp.dot(xb, w1_ref[:, c * fc:(c + 1) * fc], preferred_element_type=F32)
        u = jnp.dot(xb, w3_ref[:, c * fc:(c + 1) * fc], preferred_element_type=F32)
        gate_ref[:, c * fc:(c + 1) * fc] = (a * _sigmoid(a) * u).astype(BF16)
    f = jnp.dot(gate_ref[...], w2_ref[...], preferred_element_type=F32)
    o_ref[...] = _layer_norm(alpha * x + f, g_ref[...], b_ref[...])


def _ffn_ln(x2, w1, w3, w2, g, b, alpha, *, tm=512, fc=256):
    T, D = x2.shape
    dff = w1.shape[1]
    return pl.pallas_call(
        functools.partial(_ffn_kernel, alpha=alpha, fc=fc),
        grid=(T // tm,),
        in_specs=[
            pl.BlockSpec((tm, D), lambda i: (i, 0)),
            _const_spec((D, dff)), _const_spec((D, dff)), _const_spec((dff, D)),
            _const_spec((1, D)), _const_spec((1, D)),
        ],
        out_specs=pl.BlockSpec((tm, D), lambda i: (i, 0)),
        out_shape=jax.ShapeDtypeStruct((T, D), F32),
        scratch_shapes=[pltpu.VMEM((tm, dff), BF16)],
        compiler_params=_cparams(1),
        name="ffn_ln",
    )(x2, w1.astype(BF16), w3.astype(BF16), w2.astype(BF16), g.reshape(1, D), b.reshape(1, D))


def _glu_kernel(x_ref, w_ref, b_ref, o_ref):
    D = o_ref.shape[1]
    h = jnp.dot(x_ref[...].astype(BF16), w_ref[...], preferred_element_type=F32) + b_ref[...]
    o_ref[...] = h[:, :D] * _sigmoid(h[:, D:])


def _conv_glu(x2, w_in, b_in, *, tm=512):
    T, D = x2.shape
    return pl.pallas_call(
        _glu_kernel,
        grid=(T // tm,),
        in_specs=[pl.BlockSpec((tm, D), lambda i: (i, 0)), _const_spec((D, 2 * D)), _const_spec((1, 2 * D))],
        out_specs=pl.BlockSpec((tm, D), lambda i: (i, 0)),
        out_shape=jax.ShapeDtypeStruct((T, D), F32),
        compiler_params=_cparams(1),
        name="conv_glu",
    )(x2, w_in.astype(BF16), b_in.reshape(1, 2 * D))


def _conv_mid_kernel(x_ref, h_ref, hp_ref, hn_ref, dw_ref, dwb_ref, lg_ref, lb_ref, wo_ref, bo_ref,
                     g_ref, b_ref, o_ref, buf_ref, cv_ref, *, alpha, width, rb):
    ts = h_ref.shape[1]
    nl = buf_ref.shape[0]
    L = V7X_LANES
    i = pl.program_id(1)
    n = pl.num_programs(1)
    prev_ok = (i > 0).astype(F32)
    next_ok = (i < n - 1).astype(F32)
    for lc in range(nl):
        buf_ref[lc, 0:HALO, :] = hp_ref[0, :, lc * L:(lc + 1) * L] * prev_ok
        buf_ref[lc, HALO:HALO + ts, :] = h_ref[0, :, lc * L:(lc + 1) * L]
        buf_ref[lc, HALO + ts:, :] = hn_ref[0, :, lc * L:(lc + 1) * L] * next_ok

    pad = width // 2
    off = HALO - pad

    def lane_chunk(lc, carry):
        for r0 in range(0, ts, rb):
            win = buf_ref[lc, r0:r0 + rb + 2 * HALO, :]
            acc = jnp.zeros((rb, L), F32)
            for s in range(8):
                sh = win[s:s + rb + 2 * HALO - 8]
                for a in range((2 * HALO) // 8):
                    j = 8 * a + s - off
                    if 0 <= j < width:
                        acc = acc + sh[8 * a:8 * a + rb] * dw_ref[lc, j:j + 1, :]
            cv_ref[lc, r0:r0 + rb, :] = acc + dwb_ref[lc]
        return carry

    lax.fori_loop(0, nl, lane_chunk, 0)

    cv = jnp.concatenate([cv_ref[lc] for lc in range(nl)], axis=1)
    y = _layer_norm(cv, lg_ref[...], lb_ref[...])
    y = y * _sigmoid(y)
    hout = jnp.dot(y.astype(BF16), wo_ref[...], preferred_element_type=F32) + bo_ref[...]
    o_ref[0] = _layer_norm(alpha * x_ref[0] + hout, g_ref[...], b_ref[...])


def _conv_mid(x, h, dw, dw_b, ln_g, ln_b, w_out, b_out, g, b, alpha, *, ts=512, rb=256):
    B, S, D = x.shape
    width = dw.shape[0]
    L = V7X_LANES
    nl = D // L
    assert width // 2 <= HALO and ts % HALO == 0 and ts % rb == 0
    hblk = ts // HALO
    dw3 = jnp.pad(dw, ((0, 2 * HALO - width), (0, 0))).reshape(2 * HALO, nl, L).transpose(1, 0, 2)
    dwb3 = dw_b.reshape(nl, 1, L)
    row = lambda v: v.reshape(1, D)
    kern = functools.partial(_conv_mid_kernel, alpha=alpha, width=width, rb=rb)
    return pl.pallas_call(
        kern,
        grid=(B, S // ts),
        in_specs=[
            pl.BlockSpec((1, ts, D), lambda bb, i: (bb, i, 0)),
            pl.BlockSpec((1, ts, D), lambda bb, i: (bb, i, 0)),
            pl.BlockSpec((1, HALO, D), lambda bb, i: (bb, jnp.maximum(i * hblk - 1, 0), 0)),
            pl.BlockSpec((1, HALO, D), lambda bb, i: (bb, jnp.minimum((i + 1) * hblk, S // HALO - 1), 0)),
            _const_spec(dw3.shape), _const_spec(dwb3.shape),
            _const_spec((1, D)), _const_spec((1, D)),
            _const_spec((D, D)), _const_spec((1, D)),
            _const_spec((1, D)), _const_spec((1, D)),
        ],
        out_specs=pl.BlockSpec((1, ts, D), lambda bb, i: (bb, i, 0)),
        out_shape=jax.ShapeDtypeStruct((B, S, D), F32),
        scratch_shapes=[pltpu.VMEM((nl, ts + 2 * HALO, L), F32), pltpu.VMEM((nl, ts, L), F32)],
        compiler_params=_cparams(2),
        name="conv_mid",
    )(x, h, h, h, dw3, dwb3, row(ln_g), row(ln_b), w_out.astype(BF16), row(b_out), row(g), row(b))


def kernel(x, positions, ln_g, ln_b, mla_w_dq, mla_q_norm, mla_w_uq, mla_w_dkv, mla_kv_norm, mla_w_ukv, mla_w_o, conv_w_in, conv_b_in, conv_dw, conv_dw_b, conv_ln_g, conv_ln_b, conv_w_out, conv_b_out, ffn_w1, ffn_w3, ffn_w2):
    B, S, D = x.shape
    depth = ln_g.shape[0]
    alpha = float((2 * depth) ** 0.25)
    cos, sin = _rope_tables(positions)
    for i in range(depth):
        j = i // 2
        if i % 2 == 0:
            qT, k, vT = _mla_proj(x, cos, sin, mla_w_dq[j], mla_q_norm[j], mla_w_uq[j],
                                  mla_w_dkv[j], mla_kv_norm[j], mla_w_ukv[j], tq=256)
            o = _attention(qT, k, vT)
            x2 = _proj_ln(x.reshape(B * S, D), o.reshape(B * S, -1), mla_w_o[j],
                          ln_g[i, 0], ln_b[i, 0], alpha)
        else:
            h = _conv_glu(x.reshape(B * S, D), conv_w_in[j], conv_b_in[j]).reshape(B, S, D)
            x2 = _conv_mid(x, h, conv_dw[j], conv_dw_b[j], conv_ln_g[j], conv_ln_b[j],
                           conv_w_out[j], conv_b_out[j], ln_g[i, 0], ln_b[i, 0], alpha).reshape(B * S, D)
        x2 = _ffn_ln(x2, ffn_w1[i], ffn_w3[i], ffn_w2[i], ln_g[i, 1], ln_b[i, 1], alpha)
        x = x2.reshape(B, S, D)
    return x
```

```python
import functools
import math

import jax
import jax.numpy as jnp
from jax import lax
from jax.experimental import pallas as pl
from jax.experimental.pallas import tpu as pltpu

F32 = jnp.float32
BF16 = jnp.bfloat16

N_HEADS = 16
QK_NOPE_DIM = 64
QK_ROPE_DIM = 32
V_HEAD_DIM = 64
ROPE_THETA = 10000.0
LN_EPS = 1e-5
RMS_EPS = 1e-6

V7X_LANES = 128
V7X_VMEM_BYTES = 64 * 1024 * 1024
VMEM_LIMIT = 48 * 1024 * 1024

HALO = 16


def _cparams(n_axes):
    return pltpu.CompilerParams(
        dimension_semantics=("arbitrary",) * n_axes, vmem_limit_bytes=VMEM_LIMIT)


def _const_spec(shape):
    nd = len(shape)
    return pl.BlockSpec(shape, lambda *_: (0,) * nd, pipeline_mode=pl.Buffered(1))


def _layer_norm(z, g, b):
    mu = jnp.mean(z, axis=-1, keepdims=True)
    d = z - mu
    var = jnp.mean(d * d, axis=-1, keepdims=True)
    return d * lax.rsqrt(var + LN_EPS) * g + b


def _sigmoid(a):
    return 1.0 / (1.0 + jnp.exp(-a))


def _rope_kernel(pos_ref, inv_ref, cos_ref, sin_ref):
    ang = inv_ref[...] * pos_ref[0].astype(F32)
    cos_ref[0] = jnp.cos(ang)
    sin_ref[0] = jnp.sin(ang)


def _rope_tables(positions):
    B, S = positions.shape
    hr = QK_ROPE_DIM // 2
    inv_freq = ROPE_THETA ** (-jnp.arange(0, QK_ROPE_DIM, 2, dtype=F32) / QK_ROPE_DIM)
    return pl.pallas_call(
        _rope_kernel,
        grid=(B,),
        in_specs=[pl.BlockSpec((1, 1, S), lambda b: (b, 0, 0)), _const_spec((hr, 1))],
        out_specs=[pl.BlockSpec((1, hr, S), lambda b: (b, 0, 0))] * 2,
        out_shape=[jax.ShapeDtypeStruct((B, hr, S), F32)] * 2,
        compiler_params=_cparams(1),
        name="rope_tables",
    )(positions.reshape(B, 1, S), inv_freq.reshape(hr, 1))


def _mla_proj_kernel(x_ref, cos_ref, sin_ref, wcat_ref, qn_ref, kvn_ref, wuq_ref, wuv_ref, wuk_ref,
                     qT_ref, k_ref, vT_ref, *, ql, kvl, qscale):
    H, dn, dr, dv = N_HEADS, QK_NOPE_DIM, QK_ROPE_DIM, V_HEAD_DIM
    hr = dr // 2
    tp = x_ref.shape[1]
    xb = x_ref[0].astype(BF16)
    zT = lax.dot_general(wcat_ref[...], xb, (((1,), (1,)), ((), ())),
                         preferred_element_type=F32)
    cqT = zT[0:ql]
    ckvT = zT[ql:ql + kvl]
    krT = zT[ql + kvl:ql + kvl + dr]
    cqT = cqT * lax.rsqrt(jnp.mean(cqT * cqT, axis=0, keepdims=True) + RMS_EPS) * qn_ref[...]
    ckvT = ckvT * lax.rsqrt(jnp.mean(ckvT * ckvT, axis=0, keepdims=True) + RMS_EPS) * kvn_ref[...]
    cos = cos_ref[0]
    sin = sin_ref[0]

    qT = jnp.dot(wuq_ref[...], cqT.astype(BF16), preferred_element_type=F32)
    qT = qT.reshape(H, dn + dr, tp)
    t1 = qT[:, dn:dn + hr]
    t2 = qT[:, dn + hr:dn + dr]
    r1 = (t1 * cos - t2 * sin) * qscale
    r2 = (t2 * cos + t1 * sin) * qscale
    qn = qT[:, 0:dn] * qscale
    nqb, kwq, tq = qT_ref.shape[2:]
    for j in range(nqb):
        cols = slice(j * tq, (j + 1) * tq)
        qT_ref[0, :, j, 0:dn] = qn[:, :, cols].astype(BF16)
        qT_ref[0, :, j, dn:dn + hr] = r1[:, :, cols].astype(BF16)
        qT_ref[0, :, j, dn + hr:dn + dr] = r2[:, :, cols].astype(BF16)
        qT_ref[0, :, j, dn + dr:] = jnp.zeros((H, kwq - dn - dr, tq), BF16)

    ckvb = ckvT.astype(BF16)
    vT = jnp.dot(wuv_ref[...], ckvb, preferred_element_type=F32)
    vT_ref[0] = vT.reshape(H, dv, tp).astype(BF16)

    kw = k_ref.shape[3]
    kpad = jnp.dot(ckvT.T.astype(BF16), wuk_ref[...], preferred_element_type=F32)
    k1 = krT[0:hr]
    k2 = krT[hr:dr]
    krT_pad = jnp.concatenate(
        [jnp.zeros((dn, tp), F32), k1 * cos - k2 * sin, k2 * cos + k1 * sin,
         jnp.zeros((kw - dn - dr, tp), F32)], axis=0)
    kr_pad = krT_pad.T
    for h in range(H):
        k_ref[0, h] = (kpad[:, h * kw:(h + 1) * kw] + kr_pad).astype(BF16)


def _mla_proj(x, cos, sin, w_dq, q_norm, w_uq, w_dkv, kv_norm, w_ukv, *, tq, tp=512):
    B, S, D = x.shape
    H, dn, dr, dv = N_HEADS, QK_NOPE_DIM, QK_ROPE_DIM, V_HEAD_DIM
    hr = dr // 2
    ql = w_dq.shape[1]
    kvl = kv_norm.shape[0]
    kw = V7X_LANES
    wcatT = jnp.concatenate([w_dq, w_dkv], axis=1).T.astype(BF16)
    wuqT = w_uq.T.astype(BF16)
    w_ukv3 = w_ukv.reshape(kvl, H, dn + dv)
    wuvT = w_ukv3[:, :, dn:].reshape(kvl, H * dv).T.astype(BF16)
    wukp = jnp.pad(w_ukv3[:, :, :dn], ((0, 0), (0, 0), (0, kw - dn))).reshape(kvl, H * kw).astype(BF16)
    qscale = math.log2(math.e) / math.sqrt(dn + dr)
    kern = functools.partial(_mla_proj_kernel, ql=ql, kvl=kvl, qscale=qscale)
    return pl.pallas_call(
        kern,
        grid=(B, S // tp),
        in_specs=[
            pl.BlockSpec((1, tp, D), lambda b, i: (b, i, 0)),
            pl.BlockSpec((1, hr, tp), lambda b, i: (b, 0, i)),
            pl.BlockSpec((1, hr, tp), lambda b, i: (b, 0, i)),
            _const_spec(wcatT.shape), _const_spec((ql, 1)), _const_spec((kvl, 1)),
            _const_spec(wuqT.shape), _const_spec(wuvT.shape), _const_spec(wukp.shape),
        ],
        out_specs=[
            pl.BlockSpec((1, H, tp // tq, kw, tq), lambda b, i: (b, 0, i, 0, 0)),
            pl.BlockSpec((1, H, tp, kw), lambda b, i: (b, 0, i, 0)),
            pl.BlockSpec((1, H, dv, tp), lambda b, i: (b, 0, 0, i)),
        ],
        out_shape=[
            jax.ShapeDtypeStruct((B, H, S // tq, kw, tq), BF16),
            jax.ShapeDtypeStruct((B, H, S, kw), BF16),
            jax.ShapeDtypeStruct((B, H, dv, S), BF16),
        ],
        compiler_params=_cparams(2),
        name="mla_proj",
    )(x, cos, sin, wcatT, q_norm.reshape(ql, 1), kv_norm.reshape(kvl, 1), wuqT, wuvT, wukp)


def _attn_kernel(qT_ref, k_ref, vT_ref, o_ref, sa_ref, sb_ref, ma_ref, mb_ref, oT_ref, *, tk):
    hb, nqb, _, tq = qT_ref.shape[1:]
    S = k_ref.shape[2]
    dv = vT_ref.shape[2]
    nc = S // tk
    n_items = hb * nqb

    def stage(t1, s1_ref, m1_ref, t2, s2_ref, m2_ref):
        if t1 is not None:
            h1 = t1 // nqb
            qT = qT_ref[0, h1, t1 % nqb]
            mx = jnp.full((8, tq), -jnp.inf, F32)
        if t2 is not None:
            h2 = t2 // nqb
            m8 = jnp.broadcast_to(jnp.max(m2_ref[...], axis=0, keepdims=True), (8, tq))
            l8 = jnp.zeros((8, tq), F32)
            acc = jnp.zeros((dv, tq), F32)
        for c in range(nc):
            rows = slice(c * tk, (c + 1) * tk)
            if t1 is not None:
                s = jnp.dot(k_ref[0, h1, rows, :], qT, preferred_element_type=F32)
                s1_ref[rows, :] = s
                mx = jnp.maximum(mx, jnp.max(s.reshape(tk // 8, 8, tq), axis=0))
            if t2 is not None:
                p = jnp.exp2(s2_ref[rows, :].reshape(tk // 8, 8, tq) - m8)
                l8 = l8 + jnp.sum(p, axis=0)
                acc = acc + jnp.dot(vT_ref[0, h2, :, rows], p.reshape(tk, tq).astype(BF16),
                                    preferred_element_type=F32)
        if t1 is not None:
            m1_ref[...] = mx
        if t2 is not None:
            oT_ref[h2, t2 % nqb] = acc / jnp.sum(l8, axis=0, keepdims=True)

    stage(0, sa_ref, ma_ref, None, None, None)

    def pair(j, carry):
        stage(2 * j + 1, sb_ref, mb_ref, 2 * j, sa_ref, ma_ref)
        stage(2 * j + 2, sa_ref, ma_ref, 2 * j + 1, sb_ref, mb_ref)
        return carry

    lax.fori_loop(0, n_items // 2 - 1, pair, 0)
    stage(n_items - 1, sb_ref, mb_ref, n_items - 2, sa_ref, ma_ref)
    stage(None, None, None, n_items - 1, sb_ref, mb_ref)

    for qb in range(nqb):
        oT = jnp.concatenate([oT_ref[h, qb] for h in range(hb)], axis=0)
        o_ref[0, qb * tq:(qb + 1) * tq, :] = oT.T.astype(o_ref.dtype)


def _attention(qT, k, vT, *, tk=256, hb=2):
    B, H, nqb, kw, tq = qT.shape
    S = k.shape[2]
    dv = vT.shape[2]
    assert (hb * nqb) % 2 == 0 and hb * dv == V7X_LANES
    return pl.pallas_call(
        functools.partial(_attn_kernel, tk=tk),
        grid=(B, H // hb),
        in_specs=[
            pl.BlockSpec((1, hb, nqb, kw, tq), lambda b, h: (b, h, 0, 0, 0)),
            pl.BlockSpec((1, hb, S, kw), lambda b, h: (b, h, 0, 0)),
            pl.BlockSpec((1, hb, dv, S), lambda b, h: (b, h, 0, 0)),
        ],
        out_specs=pl.BlockSpec((1, S, hb * dv), lambda b, h: (b, 0, h)),
        out_shape=jax.ShapeDtypeStruct((B, S, H * dv), BF16),
        scratch_shapes=[
            pltpu.VMEM((S, tq), F32), pltpu.VMEM((S, tq), F32),
            pltpu.VMEM((8, tq), F32), pltpu.VMEM((8, tq), F32),
            pltpu.VMEM((hb, nqb, dv, tq), F32),
        ],
        compiler_params=_cparams(2),
        name="mla_attention",
    )(qT, k, vT)


def _proj_ln_kernel(x_ref, a_ref, w_ref, g_ref, b_ref, o_ref, *, alpha):
    h = jnp.dot(a_ref[...], w_ref[...], preferred_element_type=F32)
    o_ref[...] = _layer_norm(alpha * x_ref[...] + h, g_ref[...], b_ref[...])


def _proj_ln(x2, a2, w, g, b, alpha, *, tm=512):
    T, D = x2.shape
    K = a2.shape[1]
    return pl.pallas_call(
        functools.partial(_proj_ln_kernel, alpha=alpha),
        grid=(T // tm,),
        in_specs=[
            pl.BlockSpec((tm, D), lambda i: (i, 0)),
            pl.BlockSpec((tm, K), lambda i: (i, 0)),
            _const_spec((K, D)), _const_spec((1, D)), _const_spec((1, D)),
        ],
        out_specs=pl.BlockSpec((tm, D), lambda i: (i, 0)),
        out_shape=jax.ShapeDtypeStruct((T, D), F32),
        compiler_params=_cparams(1),
        name="out_proj_ln",
    )(x2, a2, w.astype(BF16), g.reshape(1, D), b.reshape(1, D))


def _ffn_kernel(x_ref, w1_ref, w3_ref, w2_ref, g_ref, b_ref, o_ref, gate_ref, *, alpha, fc):
    x = x_ref[...]
    xb = x.astype(BF16)
    dff = w1_ref.shape[1]
    for c in range(dff // fc):
        a = jnp.dot(xb, w1_ref[:, c * fc:(c + 1) * fc], preferred_element_type=F32)
        u = jnp.dot(xb, w3_ref[:, c * fc:(c + 1) * fc], preferred_element_type=F32)
        gate_ref[:, c * fc:(c + 1) * fc] = (a * _sigmoid(a) * u).astype(BF16)
    f = jnp.dot(gate_ref[...], w2_ref[...], preferred_element_type=F32)
    o_ref[...] = _layer_norm(alpha * x + f, g_ref[...], b_ref[...])


def _ffn_ln(x2, w1, w3, w2, g, b, alpha, *, tm=1024, fc=256):
    T, D = x2.shape
    dff = w1.shape[1]
    return pl.pallas_call(
        functools.partial(_ffn_kernel, alpha=alpha, fc=fc),
        grid=(T // tm,),
        in_specs=[
            pl.BlockSpec((tm, D), lambda i: (i, 0)),
            _const_spec((D, dff)), _const_spec((D, dff)), _const_spec((dff, D)),
            _const_spec((1, D)), _const_spec((1, D)),
        ],
        out_specs=pl.BlockSpec((tm, D), lambda i: (i, 0)),
        out_shape=jax.ShapeDtypeStruct((T, D), F32),
        scratch_shapes=[pltpu.VMEM((tm, dff), BF16)],
        compiler_params=_cparams(1),
        name="ffn_ln",
    )(x2, w1.astype(BF16), w3.astype(BF16), w2.astype(BF16), g.reshape(1, D), b.reshape(1, D))


def _glu_kernel(x_ref, w_ref, b_ref, o_ref):
    D = o_ref.shape[1]
    h = jnp.dot(x_ref[...].astype(BF16), w_ref[...], preferred_element_type=F32) + b_ref[...]
    o_ref[...] = h[:, :D] * _sigmoid(h[:, D:])


def _conv_glu(x2, w_in, b_in, *, tm=512):
    T, D = x2.shape
    return pl.pallas_call(
        _glu_kernel,
        grid=(T // tm,),
        in_specs=[pl.BlockSpec((tm, D), lambda i: (i, 0)), _const_spec((D, 2 * D)), _const_spec((1, 2 * D))],
        out_specs=pl.BlockSpec((tm, D), lambda i: (i, 0)),
        out_shape=jax.ShapeDtypeStruct((T, D), F32),
        compiler_params=_cparams(1),
        name="conv_glu",
    )(x2, w_in.astype(BF16), b_in.reshape(1, 2 * D))


def _conv_mid_kernel(x_ref, h_ref, hp_ref, hn_ref, dw_ref, dwb_ref, lg_ref, lb_ref, wo_ref, bo_ref,
                     g_ref, b_ref, o_ref, buf_ref, cv_ref, *, alpha, width, rb):
    ts = h_ref.shape[1]
    nl = buf_ref.shape[0]
    L = V7X_LANES
    i = pl.program_id(1)
    n = pl.num_programs(1)
    prev_ok = (i > 0).astype(F32)
    next_ok = (i < n - 1).astype(F32)
    for lc in range(nl):
        buf_ref[lc, 0:HALO, :] = hp_ref[0, :, lc * L:(lc + 1) * L] * prev_ok
        buf_ref[lc, HALO:HALO + ts, :] = h_ref[0, :, lc * L:(lc + 1) * L]
        buf_ref[lc, HALO + ts:, :] = hn_ref[0, :, lc * L:(lc + 1) * L] * next_ok

    pad = width // 2
    off = HALO - pad

    def lane_chunk(lc, carry):
        for r0 in range(0, ts, rb):
            acc = jnp.zeros((rb, L), F32)
            for j in range(width):
                acc = acc + buf_ref[lc, r0 + off + j:r0 + off + j + rb, :] * dw_ref[lc, j:j + 1, :]
            cv_ref[lc, r0:r0 + rb, :] = acc + dwb_ref[lc]
        return carry

    lax.fori_loop(0, nl, lane_chunk, 0)

    cv = jnp.concatenate([cv_ref[lc] for lc in range(nl)], axis=1)
    y = _layer_norm(cv, lg_ref[...], lb_ref[...])
    y = y * _sigmoid(y)
    hout = jnp.dot(y.astype(BF16), wo_ref[...], preferred_element_type=F32) + bo_ref[...]
    o_ref[0] = _layer_norm(alpha * x_ref[0] + hout, g_ref[...], b_ref[...])


def _conv_mid(x, h, dw, dw_b, ln_g, ln_b, w_out, b_out, g, b, alpha, *, ts=512, rb=256):
    B, S, D = x.shape
    width = dw.shape[0]
    L = V7X_LANES
    nl = D // L
    assert width // 2 <= HALO and ts % HALO == 0 and ts % rb == 0
    hblk = ts // HALO
    dw3 = jnp.pad(dw, ((0, 2 * HALO - width), (0, 0))).reshape(2 * HALO, nl, L).transpose(1, 0, 2)
    dwb3 = dw_b.reshape(nl, 1, L)
    row = lambda v: v.reshape(1, D)
    kern = functools.partial(_conv_mid_kernel, alpha=alpha, width=width, rb=rb)
    return pl.pallas_call(
        kern,
        grid=(B, S // ts),
        in_specs=[
            pl.BlockSpec((1, ts, D), lambda bb, i: (bb, i, 0)),
            pl.BlockSpec((1, ts, D), lambda bb, i: (bb, i, 0)),
            pl.BlockSpec((1, HALO, D), lambda bb, i: (bb, jnp.maximum(i * hblk - 1, 0), 0)),
            pl.BlockSpec((1, HALO, D), lambda bb, i: (bb, jnp.minimum((i + 1) * hblk, S // HALO - 1), 0)),
            _const_spec(dw3.shape), _const_spec(dwb3.shape),
            _const_spec((1, D)), _const_spec((1, D)),
            _const_spec((D, D)), _const_spec((1, D)),
            _const_spec((1, D)), _const_spec((1, D)),
        ],
        out_specs=pl.BlockSpec((1, ts, D), lambda bb, i: (bb, i, 0)),
        out_shape=jax.ShapeDtypeStruct((B, S, D), F32),
        scratch_shapes=[pltpu.VMEM((nl, ts + 2 * HALO, L), F32), pltpu.VMEM((nl, ts, L), F32)],
        compiler_params=_cparams(2),
        name="conv_mid",
    )(x, h, h, h, dw3, dwb3, row(ln_g), row(ln_b), w_out.astype(BF16), row(b_out), row(g), row(b))


def kernel(x, positions, ln_g, ln_b, mla_w_dq, mla_q_norm, mla_w_uq, mla_w_dkv, mla_kv_norm, mla_w_ukv, mla_w_o, conv_w_in, conv_b_in, conv_dw, conv_dw_b, conv_ln_g, conv_ln_b, conv_w_out, conv_b_out, ffn_w1, ffn_w3, ffn_w2):
    B, S, D = x.shape
    depth = ln_g.shape[0]
    alpha = float((2 * depth) ** 0.25)
    cos, sin = _rope_tables(positions)
    for i in range(depth):
        j = i // 2
        if i % 2 == 0:
            qT, k, vT = _mla_proj(x, cos, sin, mla_w_dq[j], mla_q_norm[j], mla_w_uq[j],
                                  mla_w_dkv[j], mla_kv_norm[j], mla_w_ukv[j], tq=512)
            o = _attention(qT, k, vT)
            x2 = _proj_ln(x.reshape(B * S, D), o.reshape(B * S, -1), mla_w_o[j],
                          ln_g[i, 0], ln_b[i, 0], alpha)
        else:
            h = _conv_glu(x.reshape(B * S, D), conv_w_in[j], conv_b_in[j]).reshape(B, S, D)
            x2 = _conv_mid(x, h, conv_dw[j], conv_dw_b[j], conv_ln_g[j], conv_ln_b[j],
                           conv_w_out[j], conv_b_out[j], ln_g[i, 0], ln_b[i, 0], alpha).reshape(B * S, D)
        x2 = _ffn_ln(x2, ffn_w1[i], ffn_w3[i], ffn_w2[i], ln_g[i, 1], ln_b[i, 1], alpha)
        x = x2.reshape(B, S, D)
    return x
```

```python
import functools
import math

import jax
import jax.numpy as jnp
from jax import lax
from jax.experimental import pallas as pl
from jax.experimental.pallas import tpu as pltpu

F32 = jnp.float32
BF16 = jnp.bfloat16

N_HEADS = 16
QK_NOPE_DIM = 64
QK_ROPE_DIM = 32
V_HEAD_DIM = 64
ROPE_THETA = 10000.0
LN_EPS = 1e-5
RMS_EPS = 1e-6

V7X_LANES = 128
V7X_VMEM_BYTES = 64 * 1024 * 1024
VMEM_LIMIT = 48 * 1024 * 1024

HALO = 16


def _cparams(n_axes):
    return pltpu.CompilerParams(
        dimension_semantics=("arbitrary",) * n_axes, vmem_limit_bytes=VMEM_LIMIT)


def _const_spec(shape):
    nd = len(shape)
    return pl.BlockSpec(shape, lambda *_: (0,) * nd, pipeline_mode=pl.Buffered(1))


def _layer_norm(z, g, b):
    mu = jnp.mean(z, axis=-1, keepdims=True)
    d = z - mu
    var = jnp.mean(d * d, axis=-1, keepdims=True)
    return d * lax.rsqrt(var + LN_EPS) * g + b


def _sigmoid(a):
    return 1.0 / (1.0 + jnp.exp(-a))


def _rope_kernel(pos_ref, inv_ref, cos_ref, sin_ref):
    ang = inv_ref[...] * pos_ref[0].astype(F32)
    cos_ref[0] = jnp.cos(ang)
    sin_ref[0] = jnp.sin(ang)


def _rope_tables(positions):
    B, S = positions.shape
    hr = QK_ROPE_DIM // 2
    inv_freq = ROPE_THETA ** (-jnp.arange(0, QK_ROPE_DIM, 2, dtype=F32) / QK_ROPE_DIM)
    return pl.pallas_call(
        _rope_kernel,
        grid=(B,),
        in_specs=[pl.BlockSpec((1, 1, S), lambda b: (b, 0, 0)), _const_spec((hr, 1))],
        out_specs=[pl.BlockSpec((1, hr, S), lambda b: (b, 0, 0))] * 2,
        out_shape=[jax.ShapeDtypeStruct((B, hr, S), F32)] * 2,
        compiler_params=_cparams(1),
        name="rope_tables",
    )(positions.reshape(B, 1, S), inv_freq.reshape(hr, 1))


def _mla_proj_kernel(x_ref, cos_ref, sin_ref, wcat_ref, qn_ref, kvn_ref, wuq_ref, wuv_ref, wuk_ref,
                     qT_ref, k_ref, vT_ref, *, ql, kvl, qscale):
    H, dn, dr, dv = N_HEADS, QK_NOPE_DIM, QK_ROPE_DIM, V_HEAD_DIM
    hr = dr // 2
    tp = x_ref.shape[1]
    xb = x_ref[0].astype(BF16)
    zT = lax.dot_general(wcat_ref[...], xb, (((1,), (1,)), ((), ())),
                         preferred_element_type=F32)
    cqT = zT[0:ql]
    ckvT = zT[ql:ql + kvl]
    krT = zT[ql + kvl:ql + kvl + dr]
    cqT = cqT * lax.rsqrt(jnp.mean(cqT * cqT, axis=0, keepdims=True) + RMS_EPS) * qn_ref[...]
    ckvT = ckvT * lax.rsqrt(jnp.mean(ckvT * ckvT, axis=0, keepdims=True) + RMS_EPS) * kvn_ref[...]
    cos = cos_ref[0]
    sin = sin_ref[0]

    qT = jnp.dot(wuq_ref[...], cqT.astype(BF16), preferred_element_type=F32)
    qT = qT.reshape(H, dn + dr, tp)
    t1 = qT[:, dn:dn + hr]
    t2 = qT[:, dn + hr:dn + dr]
    r1 = (t1 * cos - t2 * sin) * qscale
    r2 = (t2 * cos + t1 * sin) * qscale
    qn = qT[:, 0:dn] * qscale
    nqb, kwq, tq = qT_ref.shape[2:]
    for j in range(nqb):
        cols = slice(j * tq, (j + 1) * tq)
        qT_ref[0, :, j, 0:dn] = qn[:, :, cols].astype(BF16)
        qT_ref[0, :, j, dn:dn + hr] = r1[:, :, cols].astype(BF16)
        qT_ref[0, :, j, dn + hr:dn + dr] = r2[:, :, cols].astype(BF16)
        qT_ref[0, :, j, dn + dr:] = jnp.zeros((H, kwq - dn - dr, tq), BF16)

    ckvb = ckvT.astype(BF16)
    vT = jnp.dot(wuv_ref[...], ckvb, preferred_element_type=F32)
    vT_ref[0] = vT.reshape(H, dv, tp).astype(BF16)

    kw = k_ref.shape[3]
    kpad = jnp.dot(ckvT.T.astype(BF16), wuk_ref[...], preferred_element_type=F32)
    k1 = krT[0:hr]
    k2 = krT[hr:dr]
    krT_pad = jnp.concatenate(
        [jnp.zeros((dn, tp), F32), k1 * cos - k2 * sin, k2 * cos + k1 * sin,
         jnp.zeros((kw - dn - dr, tp), F32)], axis=0)
    kr_pad = krT_pad.T
    for h in range(H):
        k_ref[0, h] = (kpad[:, h * kw:(h + 1) * kw] + kr_pad).astype(BF16)


def _mla_proj(x, cos, sin, w_dq, q_norm, w_uq, w_dkv, kv_norm, w_ukv, *, tq, tp=1024):
    B, S, D = x.shape
    H, dn, dr, dv = N_HEADS, QK_NOPE_DIM, QK_ROPE_DIM, V_HEAD_DIM
    hr = dr // 2
    ql = w_dq.shape[1]
    kvl = kv_norm.shape[0]
    kw = V7X_LANES
    wcatT = jnp.concatenate([w_dq, w_dkv], axis=1).T.astype(BF16)
    wuqT = w_uq.T.astype(BF16)
    w_ukv3 = w_ukv.reshape(kvl, H, dn + dv)
    wuvT = w_ukv3[:, :, dn:].reshape(kvl, H * dv).T.astype(BF16)
    wukp = jnp.pad(w_ukv3[:, :, :dn], ((0, 0), (0, 0), (0, kw - dn))).reshape(kvl, H * kw).astype(BF16)
    qscale = math.log2(math.e) / math.sqrt(dn + dr)
    kern = functools.partial(_mla_proj_kernel, ql=ql, kvl=kvl, qscale=qscale)
    return pl.pallas_call(
        kern,
        grid=(B, S // tp),
        in_specs=[
            pl.BlockSpec((1, tp, D), lambda b, i: (b, i, 0)),
            pl.BlockSpec((1, hr, tp), lambda b, i: (b, 0, i)),
            pl.BlockSpec((1, hr, tp), lambda b, i: (b, 0, i)),
            _const_spec(wcatT.shape), _const_spec((ql, 1)), _const_spec((kvl, 1)),
            _const_spec(wuqT.shape), _const_spec(wuvT.shape), _const_spec(wukp.shape),
        ],
        out_specs=[
            pl.BlockSpec((1, H, tp // tq, kw, tq), lambda b, i: (b, 0, i, 0, 0)),
            pl.BlockSpec((1, H, tp, kw), lambda b, i: (b, 0, i, 0)),
            pl.BlockSpec((1, H, dv, tp), lambda b, i: (b, 0, 0, i)),
        ],
        out_shape=[
            jax.ShapeDtypeStruct((B, H, S // tq, kw, tq), BF16),
            jax.ShapeDtypeStruct((B, H, S, kw), BF16),
            jax.ShapeDtypeStruct((B, H, dv, S), BF16),
        ],
        compiler_params=_cparams(2),
        name="mla_proj",
    )(x, cos, sin, wcatT, q_norm.reshape(ql, 1), kv_norm.reshape(kvl, 1), wuqT, wuvT, wukp)


def _attn_kernel(qT_ref, k_ref, vT_ref, o_ref, sa_ref, sb_ref, ma_ref, mb_ref, oT_ref, *, tk):
    hb, nqb, _, tq = qT_ref.shape[1:]
    S = k_ref.shape[2]
    dv = vT_ref.shape[2]
    nc = S // tk
    n_items = hb * nqb

    def stage(t1, s1_ref, m1_ref, t2, s2_ref, m2_ref):
        if t1 is not None:
            h1 = t1 // nqb
            qT = qT_ref[0, h1, t1 % nqb]
            mx = jnp.full((8, tq), -jnp.inf, F32)
        if t2 is not None:
            h2 = t2 // nqb
            m8 = jnp.broadcast_to(jnp.max(m2_ref[...], axis=0, keepdims=True), (8, tq))
            l8 = jnp.zeros((8, tq), F32)
            acc = jnp.zeros((dv, tq), F32)
        for c in range(nc):
            rows = slice(c * tk, (c + 1) * tk)
            if t1 is not None:
                s = jnp.dot(k_ref[0, h1, rows, :], qT, preferred_element_type=F32)
                s1_ref[rows, :] = s
                mx = jnp.maximum(mx, jnp.max(s.reshape(tk // 8, 8, tq), axis=0))
            if t2 is not None:
                p = jnp.exp2(s2_ref[rows, :].reshape(tk // 8, 8, tq) - m8)
                l8 = l8 + jnp.sum(p, axis=0)
                acc = acc + jnp.dot(vT_ref[0, h2, :, rows], p.reshape(tk, tq).astype(BF16),
                                    preferred_element_type=F32)
        if t1 is not None:
            m1_ref[...] = mx
        if t2 is not None:
            oT_ref[h2, t2 % nqb] = acc / jnp.sum(l8, axis=0, keepdims=True)

    stage(0, sa_ref, ma_ref, None, None, None)

    def pair(j, carry):
        stage(2 * j + 1, sb_ref, mb_ref, 2 * j, sa_ref, ma_ref)
        stage(2 * j + 2, sa_ref, ma_ref, 2 * j + 1, sb_ref, mb_ref)
        return carry

    lax.fori_loop(0, n_items // 2 - 1, pair, 0)
    stage(n_items - 1, sb_ref, mb_ref, n_items - 2, sa_ref, ma_ref)
    stage(None, None, None, n_items - 1, sb_ref, mb_ref)

    for qb in range(nqb):
        oT = jnp.concatenate([oT_ref[h, qb] for h in range(hb)], axis=0)
        o_ref[0, qb * tq:(qb + 1) * tq, :] = oT.T.astype(o_ref.dtype)


def _attention(qT, k, vT, *, tk=256, hb=2):
    B, H, nqb, kw, tq = qT.shape
    S = k.shape[2]
    dv = vT.shape[2]
    assert (hb * nqb) % 2 == 0 and hb * dv == V7X_LANES
    return pl.pallas_call(
        functools.partial(_attn_kernel, tk=tk),
        grid=(B, H // hb),
        in_specs=[
            pl.BlockSpec((1, hb, nqb, kw, tq), lambda b, h: (b, h, 0, 0, 0)),
            pl.BlockSpec((1, hb, S, kw), lambda b, h: (b, h, 0, 0)),
            pl.BlockSpec((1, hb, dv, S), lambda b, h: (b, h, 0, 0)),
        ],
        out_specs=pl.BlockSpec((1, S, hb * dv), lambda b, h: (b, 0, h)),
        out_shape=jax.ShapeDtypeStruct((B, S, H * dv), BF16),
        scratch_shapes=[
            pltpu.VMEM((S, tq), F32), pltpu.VMEM((S, tq), F32),
            pltpu.VMEM((8, tq), F32), pltpu.VMEM((8, tq), F32),
            pltpu.VMEM((hb, nqb, dv, tq), F32),
        ],
        compiler_params=_cparams(2),
        name="mla_attention",
    )(qT, k, vT)


def _swiglu_ln(x1, w1_ref, w3_ref, w2_ref, g, b, gate_ref, rows, *, alpha, fc):
    xb = x1.astype(BF16)
    dff = w1_ref.shape[1]
    for c in range(dff // fc):
        cols = slice(c * fc, (c + 1) * fc)
        a = jnp.dot(xb, w1_ref[:, cols], preferred_element_type=F32)
        u = jnp.dot(xb, w3_ref[:, cols], preferred_element_type=F32)
        gate_ref[rows, cols] = (a * _sigmoid(a) * u).astype(BF16)
    f = jnp.dot(gate_ref[rows, :], w2_ref[...], preferred_element_type=F32)
    return _layer_norm(alpha * x1 + f, g, b)


def _proj_ffn_kernel(x_ref, a_ref, wo_ref, bo_ref, g1_ref, b1_ref, w1_ref, w3_ref, w2_ref, g2_ref, b2_ref,
                     o_ref, gate_ref, *, alpha, fc, sub):
    for r0 in range(0, x_ref.shape[0], sub):
        rows = slice(r0, r0 + sub)
        h = jnp.dot(a_ref[rows, :], wo_ref[...], preferred_element_type=F32) + bo_ref[...]
        x1 = _layer_norm(alpha * x_ref[rows, :] + h, g1_ref[...], b1_ref[...])
        o_ref[rows, :] = _swiglu_ln(x1, w1_ref, w3_ref, w2_ref, g2_ref[...], b2_ref[...],
                                    gate_ref, rows, alpha=alpha, fc=fc)


def _proj_ffn_ln(x2, a2, wo, bo, g1, b1, w1, w3, w2, g2, b2, alpha, *, tm=1024, fc=256, sub=256):
    T, D = x2.shape
    K = a2.shape[1]
    dff = w1.shape[1]
    assert dff % fc == 0 and tm % sub == 0
    row = lambda v: v.reshape(1, D)
    return pl.pallas_call(
        functools.partial(_proj_ffn_kernel, alpha=alpha, fc=fc, sub=sub),
        grid=(T // tm,),
        in_specs=[
            pl.BlockSpec((tm, D), lambda i: (i, 0)),
            pl.BlockSpec((tm, K), lambda i: (i, 0)),
            _const_spec((K, D)), _const_spec((1, D)), _const_spec((1, D)), _const_spec((1, D)),
            _const_spec((D, dff)), _const_spec((D, dff)), _const_spec((dff, D)),
            _const_spec((1, D)), _const_spec((1, D)),
        ],
        out_specs=pl.BlockSpec((tm, D), lambda i: (i, 0)),
        out_shape=jax.ShapeDtypeStruct((T, D), F32),
        scratch_shapes=[pltpu.VMEM((tm, dff), BF16)],
        compiler_params=_cparams(1),
        name="proj_ffn_ln",
    )(x2, a2, wo.astype(BF16), row(bo), row(g1), row(b1), w1.astype(BF16), w3.astype(BF16), w2.astype(BF16),
      row(g2), row(b2))


def _glu_kernel(x_ref, w_ref, b_ref, o_ref):
    D = o_ref.shape[1]
    h = jnp.dot(x_ref[...].astype(BF16), w_ref[...], preferred_element_type=F32) + b_ref[...]
    o_ref[...] = h[:, :D] * _sigmoid(h[:, D:])


def _conv_glu(x2, w_in, b_in, *, tm=1024):
    T, D = x2.shape
    return pl.pallas_call(
        _glu_kernel,
        grid=(T // tm,),
        in_specs=[pl.BlockSpec((tm, D), lambda i: (i, 0)), _const_spec((D, 2 * D)), _const_spec((1, 2 * D))],
        out_specs=pl.BlockSpec((tm, D), lambda i: (i, 0)),
        out_shape=jax.ShapeDtypeStruct((T, D), F32),
        compiler_params=_cparams(1),
        name="conv_glu",
    )(x2, w_in.astype(BF16), b_in.reshape(1, 2 * D))


def _conv_dw_kernel(h_ref, hp_ref, hn_ref, dw_ref, dwb_ref, lg_ref, lb_ref, o_ref, buf_ref, cv_ref,
                    *, width, rb):
    ts = h_ref.shape[1]
    nl = buf_ref.shape[0]
    L = V7X_LANES
    i = pl.program_id(1)
    n = pl.num_programs(1)
    prev_ok = (i > 0).astype(F32)
    next_ok = (i < n - 1).astype(F32)
    for lc in range(nl):
        buf_ref[lc, 0:HALO, :] = hp_ref[0, :, lc * L:(lc + 1) * L] * prev_ok
        buf_ref[lc, HALO:HALO + ts, :] = h_ref[0, :, lc * L:(lc + 1) * L]
        buf_ref[lc, HALO + ts:, :] = hn_ref[0, :, lc * L:(lc + 1) * L] * next_ok

    pad = width // 2
    off = HALO - pad

    def lane_chunk(lc, carry):
        for r0 in range(0, ts, rb):
            acc = jnp.zeros((rb, L), F32)
            for j in range(width):
                acc = acc + buf_ref[lc, r0 + off + j:r0 + off + j + rb, :] * dw_ref[lc, j:j + 1, :]
            cv_ref[lc, r0:r0 + rb, :] = acc + dwb_ref[lc]
        return carry

    lax.fori_loop(0, nl, lane_chunk, 0)

    cv = jnp.concatenate([cv_ref[lc] for lc in range(nl)], axis=1)
    y = _layer_norm(cv, lg_ref[...], lb_ref[...])
    o_ref[0] = (y * _sigmoid(y)).astype(o_ref.dtype)


def _conv_dw(h, dw, dw_b, ln_g, ln_b, *, ts=512, rb=256):
    B, S, D = h.shape
    width = dw.shape[0]
    L = V7X_LANES
    nl = D // L
    assert width // 2 <= HALO and ts % HALO == 0 and ts % rb == 0
    hblk = ts // HALO
    dw3 = jnp.pad(dw, ((0, 2 * HALO - width), (0, 0))).reshape(2 * HALO, nl, L).transpose(1, 0, 2)
    dwb3 = dw_b.reshape(nl, 1, L)
    row = lambda v: v.reshape(1, D)
    kern = functools.partial(_conv_dw_kernel, width=width, rb=rb)
    return pl.pallas_call(
        kern,
        grid=(B, S // ts),
        in_specs=[
            pl.BlockSpec((1, ts, D), lambda bb, i: (bb, i, 0)),
            pl.BlockSpec((1, HALO, D), lambda bb, i: (bb, jnp.maximum(i * hblk - 1, 0), 0)),
            pl.BlockSpec((1, HALO, D), lambda bb, i: (bb, jnp.minimum((i + 1) * hblk, S // HALO - 1), 0)),
            _const_spec(dw3.shape), _const_spec(dwb3.shape),
            _const_spec((1, D)), _const_spec((1, D)),
        ],
        out_specs=pl.BlockSpec((1, ts, D), lambda bb, i: (bb, i, 0)),
        out_shape=jax.ShapeDtypeStruct((B, S, D), BF16),
        scratch_shapes=[pltpu.VMEM((nl, ts + 2 * HALO, L), F32), pltpu.VMEM((nl, ts, L), F32)],
        compiler_params=_cparams(2),
        name="conv_dw",
    )(h, h, h, dw3, dwb3, row(ln_g), row(ln_b))


def kernel(x, positions, ln_g, ln_b, mla_w_dq, mla_q_norm, mla_w_uq, mla_w_dkv, mla_kv_norm, mla_w_ukv, mla_w_o, conv_w_in, conv_b_in, conv_dw, conv_dw_b, conv_ln_g, conv_ln_b, conv_w_out, conv_b_out, ffn_w1, ffn_w3, ffn_w2):
    B, S, D = x.shape
    depth = ln_g.shape[0]
    alpha = float((2 * depth) ** 0.25)
    cos, sin = _rope_tables(positions)
    x2 = x.reshape(B * S, D)
    for i in range(depth):
        j = i // 2
        if i % 2 == 0:
            qT, k, vT = _mla_proj(x2.reshape(B, S, D), cos, sin, mla_w_dq[j], mla_q_norm[j], mla_w_uq[j],
                                  mla_w_dkv[j], mla_kv_norm[j], mla_w_ukv[j], tq=512)
            a = _attention(qT, k, vT)
            wo, bo = mla_w_o[j], jnp.zeros((D,), F32)
        else:
            h = _conv_glu(x2, conv_w_in[j], conv_b_in[j]).reshape(B, S, D)
            a = _conv_dw(h, conv_dw[j], conv_dw_b[j], conv_ln_g[j], conv_ln_b[j])
            wo, bo = conv_w_out[j], conv_b_out[j]
        x2 = _proj_ffn_ln(x2, a.reshape(B * S, -1), wo, bo, ln_g[i, 0], ln_b[i, 0],
                          ffn_w1[i], ffn_w3[i], ffn_w2[i], ln_g[i, 1], ln_b[i, 1], alpha)
    return x2.reshape(B, S, D)
```

```python
import functools
import math

import jax
import jax.numpy as jnp
from jax import lax
from jax.experimental import pallas as pl
from jax.experimental.pallas import tpu as pltpu

F32 = jnp.float32
BF16 = jnp.bfloat16

N_HEADS = 16
QK_NOPE_DIM = 64
QK_ROPE_DIM = 32
V_HEAD_DIM = 64
ROPE_THETA = 10000.0
LN_EPS = 1e-5
RMS_EPS = 1e-6

V7X_LANES = 128
V7X_VMEM_BYTES = 64 * 1024 * 1024
VMEM_LIMIT = 48 * 1024 * 1024

HALO = 16


def _cparams(n_axes):
    return pltpu.CompilerParams(
        dimension_semantics=("arbitrary",) * n_axes, vmem_limit_bytes=VMEM_LIMIT)


def _const_spec(shape):
    nd = len(shape)
    return pl.BlockSpec(shape, lambda *_: (0,) * nd, pipeline_mode=pl.Buffered(1))


def _layer_norm(z, g, b):
    mu = jnp.mean(z, axis=-1, keepdims=True)
    d = z - mu
    var = jnp.mean(d * d, axis=-1, keepdims=True)
    return d * lax.rsqrt(var + LN_EPS) * g + b


def _sigmoid(a):
    return 1.0 / (1.0 + jnp.exp(-a))


def _rope_kernel(pos_ref, inv_ref, cos_ref, sin_ref):
    ang = inv_ref[...] * pos_ref[0].astype(F32)
    cos_ref[0] = jnp.cos(ang)
    sin_ref[0] = jnp.sin(ang)


def _rope_tables(positions):
    B, S = positions.shape
    hr = QK_ROPE_DIM // 2
    inv_freq = ROPE_THETA ** (-jnp.arange(0, QK_ROPE_DIM, 2, dtype=F32) / QK_ROPE_DIM)
    return pl.pallas_call(
        _rope_kernel,
        grid=(B,),
        in_specs=[pl.BlockSpec((1, 1, S), lambda b: (b, 0, 0)), _const_spec((hr, 1))],
        out_specs=[pl.BlockSpec((1, hr, S), lambda b: (b, 0, 0))] * 2,
        out_shape=[jax.ShapeDtypeStruct((B, hr, S), F32)] * 2,
        compiler_params=_cparams(1),
        name="rope_tables",
    )(positions.reshape(B, 1, S), inv_freq.reshape(hr, 1))


def _mla_proj_kernel(x_ref, cos_ref, sin_ref, wcat_ref, qn_ref, kvn_ref, wuq_ref, wuv_ref, wuk_ref,
                     qT_ref, k_ref, vT_ref, *, ql, kvl, qscale):
    H, dn, dr, dv = N_HEADS, QK_NOPE_DIM, QK_ROPE_DIM, V_HEAD_DIM
    hr = dr // 2
    tp = x_ref.shape[1]
    xb = x_ref[0].astype(BF16)
    zT = lax.dot_general(wcat_ref[...], xb, (((1,), (1,)), ((), ())),
                         preferred_element_type=F32)
    cqT = zT[0:ql]
    ckvT = zT[ql:ql + kvl]
    krT = zT[ql + kvl:ql + kvl + dr]
    cqT = cqT * lax.rsqrt(jnp.mean(cqT * cqT, axis=0, keepdims=True) + RMS_EPS) * qn_ref[...]
    ckvT = ckvT * lax.rsqrt(jnp.mean(ckvT * ckvT, axis=0, keepdims=True) + RMS_EPS) * kvn_ref[...]
    cos = cos_ref[0]
    sin = sin_ref[0]

    qT = jnp.dot(wuq_ref[...], cqT.astype(BF16), preferred_element_type=F32)
    qT = qT.reshape(H, dn + dr, tp)
    t1 = qT[:, dn:dn + hr]
    t2 = qT[:, dn + hr:dn + dr]
    r1 = (t1 * cos - t2 * sin) * qscale
    r2 = (t2 * cos + t1 * sin) * qscale
    qn = qT[:, 0:dn] * qscale
    nqb, kwq, tq = qT_ref.shape[2:]
    for j in range(nqb):
        cols = slice(j * tq, (j + 1) * tq)
        qT_ref[0, :, j, 0:dn] = qn[:, :, cols].astype(BF16)
        qT_ref[0, :, j, dn:dn + hr] = r1[:, :, cols].astype(BF16)
        qT_ref[0, :, j, dn + hr:dn + dr] = r2[:, :, cols].astype(BF16)
        qT_ref[0, :, j, dn + dr:] = jnp.zeros((H, kwq - dn - dr, tq), BF16)

    ckvb = ckvT.astype(BF16)
    vT = jnp.dot(wuv_ref[...], ckvb, preferred_element_type=F32)
    vT_ref[0] = vT.reshape(H, dv, tp).astype(BF16)

    kw = k_ref.shape[3]
    kpad = jnp.dot(ckvT.T.astype(BF16), wuk_ref[...], preferred_element_type=F32)
    k1 = krT[0:hr]
    k2 = krT[hr:dr]
    krT_pad = jnp.concatenate(
        [jnp.zeros((dn, tp), F32), k1 * cos - k2 * sin, k2 * cos + k1 * sin,
         jnp.zeros((kw - dn - dr, tp), F32)], axis=0)
    kr_pad = krT_pad.T
    for h in range(H):
        k_ref[0, h] = (kpad[:, h * kw:(h + 1) * kw] + kr_pad).astype(BF16)


def _mla_proj(x, cos, sin, w_dq, q_norm, w_uq, w_dkv, kv_norm, w_ukv, *, tq, tp=1024):
    B, S, D = x.shape
    H, dn, dr, dv = N_HEADS, QK_NOPE_DIM, QK_ROPE_DIM, V_HEAD_DIM
    hr = dr // 2
    ql = w_dq.shape[1]
    kvl = kv_norm.shape[0]
    kw = V7X_LANES
    wcatT = jnp.concatenate([w_dq, w_dkv], axis=1).T.astype(BF16)
    wuqT = w_uq.T.astype(BF16)
    w_ukv3 = w_ukv.reshape(kvl, H, dn + dv)
    wuvT = w_ukv3[:, :, dn:].reshape(kvl, H * dv).T.astype(BF16)
    wukp = jnp.pad(w_ukv3[:, :, :dn], ((0, 0), (0, 0), (0, kw - dn))).reshape(kvl, H * kw).astype(BF16)
    qscale = math.log2(math.e) / math.sqrt(dn + dr)
    kern = functools.partial(_mla_proj_kernel, ql=ql, kvl=kvl, qscale=qscale)
    return pl.pallas_call(
        kern,
        grid=(B, S // tp),
        in_specs=[
            pl.BlockSpec((1, tp, D), lambda b, i: (b, i, 0)),
            pl.BlockSpec((1, hr, tp), lambda b, i: (b, 0, i)),
            pl.BlockSpec((1, hr, tp), lambda b, i: (b, 0, i)),
            _const_spec(wcatT.shape), _const_spec((ql, 1)), _const_spec((kvl, 1)),
            _const_spec(wuqT.shape), _const_spec(wuvT.shape), _const_spec(wukp.shape),
        ],
        out_specs=[
            pl.BlockSpec((1, H, tp // tq, kw, tq), lambda b, i: (b, 0, i, 0, 0)),
            pl.BlockSpec((1, H, tp, kw), lambda b, i: (b, 0, i, 0)),
            pl.BlockSpec((1, H, dv, tp), lambda b, i: (b, 0, 0, i)),
        ],
        out_shape=[
            jax.ShapeDtypeStruct((B, H, S // tq, kw, tq), BF16),
            jax.ShapeDtypeStruct((B, H, S, kw), BF16),
            jax.ShapeDtypeStruct((B, H, dv, S), BF16),
        ],
        compiler_params=_cparams(2),
        name="mla_proj",
    )(x, cos, sin, wcatT, q_norm.reshape(ql, 1), kv_norm.reshape(kvl, 1), wuqT, wuvT, wukp)


def _attn_kernel(qT_ref, k_ref, vT_ref, o_ref, sa_ref, sb_ref, ma_ref, mb_ref, oT_ref, *, tk):
    hb, nqb, _, tq = qT_ref.shape[1:]
    S = k_ref.shape[2]
    dv = vT_ref.shape[2]
    nc = S // tk
    n_items = hb * nqb

    def stage(t1, s1_ref, m1_ref, t2, s2_ref, m2_ref):
        if t1 is not None:
            h1 = t1 // nqb
            qT = qT_ref[0, h1, t1 % nqb]
            mx = jnp.full((8, tq), -jnp.inf, F32)
        if t2 is not None:
            h2 = t2 // nqb
            m8 = jnp.broadcast_to(jnp.max(m2_ref[...], axis=0, keepdims=True), (8, tq))
            l8 = jnp.zeros((8, tq), F32)
            acc = jnp.zeros((dv, tq), F32)
        for c in range(nc):
            rows = slice(c * tk, (c + 1) * tk)
            if t1 is not None:
                s = jnp.dot(k_ref[0, h1, rows, :], qT, preferred_element_type=F32)
                s1_ref[rows, :] = s
                mx = jnp.maximum(mx, jnp.max(s.reshape(tk // 8, 8, tq), axis=0))
            if t2 is not None:
                p = jnp.exp2(s2_ref[rows, :].reshape(tk // 8, 8, tq) - m8)
                l8 = l8 + jnp.sum(p, axis=0)
                acc = acc + jnp.dot(vT_ref[0, h2, :, rows], p.reshape(tk, tq).astype(BF16),
                                    preferred_element_type=F32)
        if t1 is not None:
            m1_ref[...] = mx
        if t2 is not None:
            oT_ref[h2, t2 % nqb] = acc / jnp.sum(l8, axis=0, keepdims=True)

    stage(0, sa_ref, ma_ref, None, None, None)

    def pair(j, carry):
        stage(2 * j + 1, sb_ref, mb_ref, 2 * j, sa_ref, ma_ref)
        stage(2 * j + 2, sa_ref, ma_ref, 2 * j + 1, sb_ref, mb_ref)
        return carry

    lax.fori_loop(0, n_items // 2 - 1, pair, 0)
    stage(n_items - 1, sb_ref, mb_ref, n_items - 2, sa_ref, ma_ref)
    stage(None, None, None, n_items - 1, sb_ref, mb_ref)

    for qb in range(nqb):
        oT = jnp.concatenate([oT_ref[h, qb] for h in range(hb)], axis=0)
        o_ref[0, qb * tq:(qb + 1) * tq, :] = oT.T.astype(o_ref.dtype)


def _attention(qT, k, vT, *, tk=256, hb=2):
    B, H, nqb, kw, tq = qT.shape
    S = k.shape[2]
    dv = vT.shape[2]
    assert (hb * nqb) % 2 == 0 and hb * dv == V7X_LANES
    return pl.pallas_call(
        functools.partial(_attn_kernel, tk=tk),
        grid=(B, H // hb),
        in_specs=[
            pl.BlockSpec((1, hb, nqb, kw, tq), lambda b, h: (b, h, 0, 0, 0)),
            pl.BlockSpec((1, hb, S, kw), lambda b, h: (b, h, 0, 0)),
            pl.BlockSpec((1, hb, dv, S), lambda b, h: (b, h, 0, 0)),
        ],
        out_specs=pl.BlockSpec((1, S, hb * dv), lambda b, h: (b, 0, h)),
        out_shape=jax.ShapeDtypeStruct((B, S, H * dv), BF16),
        scratch_shapes=[
            pltpu.VMEM((S, tq), F32), pltpu.VMEM((S, tq), F32),
            pltpu.VMEM((8, tq), F32), pltpu.VMEM((8, tq), F32),
            pltpu.VMEM((hb, nqb, dv, tq), F32),
        ],
        compiler_params=_cparams(2),
        name="mla_attention",
    )(qT, k, vT)


def _swiglu_ln(x1, w1_ref, w3_ref, w2_ref, g, b, gate_ref, rows, *, alpha, fc):
    xb = x1.astype(BF16)
    dff = w1_ref.shape[1]
    for c in range(dff // fc):
        cols = slice(c * fc, (c + 1) * fc)
        a = jnp.dot(xb, w1_ref[:, cols], preferred_element_type=F32)
        u = jnp.dot(xb, w3_ref[:, cols], preferred_element_type=F32)
        gate_ref[rows, cols] = (a * _sigmoid(a) * u).astype(BF16)
    f = jnp.dot(gate_ref[rows, :], w2_ref[...], preferred_element_type=F32)
    return _layer_norm(alpha * x1 + f, g, b)


def _ffn_kernel(x_ref, w1_ref, w3_ref, w2_ref, g_ref, b_ref, o_ref, gate_ref, *, alpha, fc, sub):
    for r0 in range(0, x_ref.shape[0], sub):
        rows = slice(r0, r0 + sub)
        o_ref[rows, :] = _swiglu_ln(x_ref[rows, :], w1_ref, w3_ref, w2_ref, g_ref[...], b_ref[...],
                                    gate_ref, rows, alpha=alpha, fc=fc)


def _proj_ffn_kernel(x_ref, a_ref, wo_ref, g1_ref, b1_ref, w1_ref, w3_ref, w2_ref, g2_ref, b2_ref,
                     o_ref, gate_ref, *, alpha, fc, sub):
    for r0 in range(0, x_ref.shape[0], sub):
        rows = slice(r0, r0 + sub)
        h = jnp.dot(a_ref[rows, :], wo_ref[...], preferred_element_type=F32)
        x1 = _layer_norm(alpha * x_ref[rows, :] + h, g1_ref[...], b1_ref[...])
        o_ref[rows, :] = _swiglu_ln(x1, w1_ref, w3_ref, w2_ref, g2_ref[...], b2_ref[...],
                                    gate_ref, rows, alpha=alpha, fc=fc)


def _ffn_ln(x2, w1, w3, w2, g, b, alpha, *, tm=1024, fc=256, sub=256):
    T, D = x2.shape
    dff = w1.shape[1]
    assert dff % fc == 0 and tm % sub == 0
    return pl.pallas_call(
        functools.partial(_ffn_kernel, alpha=alpha, fc=fc, sub=sub),
        grid=(T // tm,),
        in_specs=[
            pl.BlockSpec((tm, D), lambda i: (i, 0)),
            _const_spec((D, dff)), _const_spec((D, dff)), _const_spec((dff, D)),
            _const_spec((1, D)), _const_spec((1, D)),
        ],
        out_specs=pl.BlockSpec((tm, D), lambda i: (i, 0)),
        out_shape=jax.ShapeDtypeStruct((T, D), F32),
        scratch_shapes=[pltpu.VMEM((tm, dff), BF16)],
        compiler_params=_cparams(1),
        name="ffn_ln",
    )(x2, w1.astype(BF16), w3.astype(BF16), w2.astype(BF16), g.reshape(1, D), b.reshape(1, D))


def _proj_ffn_ln(x2, a2, wo, g1, b1, w1, w3, w2, g2, b2, alpha, *, tm=1024, fc=256, sub=256):
    T, D = x2.shape
    K = a2.shape[1]
    dff = w1.shape[1]
    assert dff % fc == 0 and tm % sub == 0
    row = lambda v: v.reshape(1, D)
    return pl.pallas_call(
        functools.partial(_proj_ffn_kernel, alpha=alpha, fc=fc, sub=sub),
        grid=(T // tm,),
        in_specs=[
            pl.BlockSpec((tm, D), lambda i: (i, 0)),
            pl.BlockSpec((tm, K), lambda i: (i, 0)),
            _const_spec((K, D)), _const_spec((1, D)), _const_spec((1, D)),
            _const_spec((D, dff)), _const_spec((D, dff)), _const_spec((dff, D)),
            _const_spec((1, D)), _const_spec((1, D)),
        ],
        out_specs=pl.BlockSpec((tm, D), lambda i: (i, 0)),
        out_shape=jax.ShapeDtypeStruct((T, D), F32),
        scratch_shapes=[pltpu.VMEM((tm, dff), BF16)],
        compiler_params=_cparams(1),
        name="proj_ffn_ln",
    )(x2, a2, wo.astype(BF16), row(g1), row(b1), w1.astype(BF16), w3.astype(BF16), w2.astype(BF16),
      row(g2), row(b2))


def _glu_kernel(x_ref, w_ref, b_ref, o_ref):
    D = o_ref.shape[1]
    h = jnp.dot(x_ref[...].astype(BF16), w_ref[...], preferred_element_type=F32) + b_ref[...]
    o_ref[...] = h[:, :D] * _sigmoid(h[:, D:])


def _conv_glu(x2, w_in, b_in, *, tm=1024):
    T, D = x2.shape
    return pl.pallas_call(
        _glu_kernel,
        grid=(T // tm,),
        in_specs=[pl.BlockSpec((tm, D), lambda i: (i, 0)), _const_spec((D, 2 * D)), _const_spec((1, 2 * D))],
        out_specs=pl.BlockSpec((tm, D), lambda i: (i, 0)),
        out_shape=jax.ShapeDtypeStruct((T, D), F32),
        compiler_params=_cparams(1),
        name="conv_glu",
    )(x2, w_in.astype(BF16), b_in.reshape(1, 2 * D))


def _conv_mid_kernel(x_ref, h_ref, hp_ref, hn_ref, dw_ref, dwb_ref, lg_ref, lb_ref, wo_ref, bo_ref,
                     g_ref, b_ref, o_ref, buf_ref, cv_ref, *, alpha, width, rb):
    ts = h_ref.shape[1]
    nl = buf_ref.shape[0]
    L = V7X_LANES
    i = pl.program_id(1)
    n = pl.num_programs(1)
    prev_ok = (i > 0).astype(F32)
    next_ok = (i < n - 1).astype(F32)
    for lc in range(nl):
        buf_ref[lc, 0:HALO, :] = hp_ref[0, :, lc * L:(lc + 1) * L] * prev_ok
        buf_ref[lc, HALO:HALO + ts, :] = h_ref[0, :, lc * L:(lc + 1) * L]
        buf_ref[lc, HALO + ts:, :] = hn_ref[0, :, lc * L:(lc + 1) * L] * next_ok

    pad = width // 2
    off = HALO - pad

    def lane_chunk(lc, carry):
        for r0 in range(0, ts, rb):
            acc = jnp.zeros((rb, L), F32)
            for j in range(width):
                acc = acc + buf_ref[lc, r0 + off + j:r0 + off + j + rb, :] * dw_ref[lc, j:j + 1, :]
            cv_ref[lc, r0:r0 + rb, :] = acc + dwb_ref[lc]
        return carry

    lax.fori_loop(0, nl, lane_chunk, 0)

    cv = jnp.concatenate([cv_ref[lc] for lc in range(nl)], axis=1)
    y = _layer_norm(cv, lg_ref[...], lb_ref[...])
    y = y * _sigmoid(y)
    hout = jnp.dot(y.astype(BF16), wo_ref[...], preferred_element_type=F32) + bo_ref[...]
    o_ref[0] = _layer_norm(alpha * x_ref[0] + hout, g_ref[...], b_ref[...])


def _conv_mid(x, h, dw, dw_b, ln_g, ln_b, w_out, b_out, g, b, alpha, *, ts=512, rb=256):
    B, S, D = x.shape
    width = dw.shape[0]
    L = V7X_LANES
    nl = D // L
    assert width // 2 <= HALO and ts % HALO == 0 and ts % rb == 0
    hblk = ts // HALO
    dw3 = jnp.pad(dw, ((0, 2 * HALO - width), (0, 0))).reshape(2 * HALO, nl, L).transpose(1, 0, 2)
    dwb3 = dw_b.reshape(nl, 1, L)
    row = lambda v: v.reshape(1, D)
    kern = functools.partial(_conv_mid_kernel, alpha=alpha, width=width, rb=rb)
    return pl.pallas_call(
        kern,
        grid=(B, S // ts),
        in_specs=[
            pl.BlockSpec((1, ts, D), lambda bb, i: (bb, i, 0)),
            pl.BlockSpec((1, ts, D), lambda bb, i: (bb, i, 0)),
            pl.BlockSpec((1, HALO, D), lambda bb, i: (bb, jnp.maximum(i * hblk - 1, 0), 0)),
            pl.BlockSpec((1, HALO, D), lambda bb, i: (bb, jnp.minimum((i + 1) * hblk, S // HALO - 1), 0)),
            _const_spec(dw3.shape), _const_spec(dwb3.shape),
            _const_spec((1, D)), _const_spec((1, D)),
            _const_spec((D, D)), _const_spec((1, D)),
            _const_spec((1, D)), _const_spec((1, D)),
        ],
        out_specs=pl.BlockSpec((1, ts, D), lambda bb, i: (bb, i, 0)),
        out_shape=jax.ShapeDtypeStruct((B, S, D), F32),
        scratch_shapes=[pltpu.VMEM((nl, ts + 2 * HALO, L), F32), pltpu.VMEM((nl, ts, L), F32)],
        compiler_params=_cparams(2),
        name="conv_mid",
    )(x, h, h, h, dw3, dwb3, row(ln_g), row(ln_b), w_out.astype(BF16), row(b_out), row(g), row(b))


def kernel(x, positions, ln_g, ln_b, mla_w_dq, mla_q_norm, mla_w_uq, mla_w_dkv, mla_kv_norm, mla_w_ukv, mla_w_o, conv_w_in, conv_b_in, conv_dw, conv_dw_b, conv_ln_g, conv_ln_b, conv_w_out, conv_b_out, ffn_w1, ffn_w3, ffn_w2):
    B, S, D = x.shape
    depth = ln_g.shape[0]
    alpha = float((2 * depth) ** 0.25)
    cos, sin = _rope_tables(positions)
    x2 = x.reshape(B * S, D)
    for i in range(depth):
        j = i // 2
        ffn = (ffn_w1[i], ffn_w3[i], ffn_w2[i], ln_g[i, 1], ln_b[i, 1], alpha)
        if i % 2 == 0:
            qT, k, vT = _mla_proj(x2.reshape(B, S, D), cos, sin, mla_w_dq[j], mla_q_norm[j], mla_w_uq[j],
                                  mla_w_dkv[j], mla_kv_norm[j], mla_w_ukv[j], tq=512)
            o = _attention(qT, k, vT)
            x2 = _proj_ffn_ln(x2, o.reshape(B * S, -1), mla_w_o[j], ln_g[i, 0], ln_b[i, 0], *ffn)
        else:
            h = _conv_glu(x2, conv_w_in[j], conv_b_in[j]).reshape(B, S, D)
            x1 = _conv_mid(x2.reshape(B, S, D), h, conv_dw[j], conv_dw_b[j], conv_ln_g[j], conv_ln_b[j],
                           conv_w_out[j], conv_b_out[j], ln_g[i, 0], ln_b[i, 0], alpha)
            x2 = _ffn_ln(x1.reshape(B * S, D), *ffn)
    return x2.reshape(B, S, D)
```

```python
import functools
import math

import jax
import jax.numpy as jnp
from jax import lax
from jax.experimental import pallas as pl
from jax.experimental.pallas import tpu as pltpu

F32 = jnp.float32
BF16 = jnp.bfloat16

N_HEADS = 16
QK_NOPE_DIM = 64
QK_ROPE_DIM = 32
V_HEAD_DIM = 64
ROPE_THETA = 10000.0
LN_EPS = 1e-5
RMS_EPS = 1e-6

V7X_LANES = 128
V7X_VMEM_BYTES = 64 * 1024 * 1024
VMEM_LIMIT = (V7X_VMEM_BYTES * 3) // 4

HALO = 16


def _cparams(n_axes):
    return pltpu.CompilerParams(
        dimension_semantics=("arbitrary",) * n_axes, vmem_limit_bytes=VMEM_LIMIT)


def _const_spec(shape):
    nd = len(shape)
    return pl.BlockSpec(shape, lambda *_: (0,) * nd, pipeline_mode=pl.Buffered(1))


def _layer_norm(z, g, b):
    mu = jnp.mean(z, axis=-1, keepdims=True)
    d = z - mu
    var = jnp.mean(d * d, axis=-1, keepdims=True)
    return d * lax.rsqrt(var + LN_EPS) * g + b


def _sigmoid(a):
    return 1.0 / (1.0 + jnp.exp(-a))


def _rope_kernel(pos_ref, inv_ref, cos_ref, sin_ref):
    ang = inv_ref[...] * pos_ref[0].astype(F32)
    cos_ref[0] = jnp.cos(ang)
    sin_ref[0] = jnp.sin(ang)


def _rope_tables(positions):
    B, S = positions.shape
    hr = QK_ROPE_DIM // 2
    inv_freq = ROPE_THETA ** (-jnp.arange(0, QK_ROPE_DIM, 2, dtype=F32) / QK_ROPE_DIM)
    return pl.pallas_call(
        _rope_kernel,
        grid=(B,),
        in_specs=[pl.BlockSpec((1, 1, S), lambda b: (b, 0, 0)), _const_spec((hr, 1))],
        out_specs=[pl.BlockSpec((1, hr, S), lambda b: (b, 0, 0))] * 2,
        out_shape=[jax.ShapeDtypeStruct((B, hr, S), F32)] * 2,
        compiler_params=_cparams(1),
        name="rope_tables",
    )(positions.reshape(B, 1, S), inv_freq.reshape(hr, 1))


def _mla_proj_kernel(x_ref, cos_ref, sin_ref, wcat_ref, qn_ref, kvn_ref, wuq_ref, wuv_ref, wuk_ref,
                     qT_ref, k_ref, vT_ref, *, ql, kvl, qscale):
    H, dn, dr, dv = N_HEADS, QK_NOPE_DIM, QK_ROPE_DIM, V_HEAD_DIM
    hr = dr // 2
    tp = x_ref.shape[1]
    xb = x_ref[0].astype(BF16)
    zT = lax.dot_general(wcat_ref[...], xb, (((1,), (1,)), ((), ())),
                         preferred_element_type=F32)
    cqT = zT[0:ql]
    ckvT = zT[ql:ql + kvl]
    krT = zT[ql + kvl:ql + kvl + dr]
    cqT = cqT * lax.rsqrt(jnp.mean(cqT * cqT, axis=0, keepdims=True) + RMS_EPS) * qn_ref[...]
    ckvT = ckvT * lax.rsqrt(jnp.mean(ckvT * ckvT, axis=0, keepdims=True) + RMS_EPS) * kvn_ref[...]
    cos = cos_ref[0]
    sin = sin_ref[0]

    qT = jnp.dot(wuq_ref[...], cqT.astype(BF16), preferred_element_type=F32)
    qT = qT.reshape(H, dn + dr, tp)
    t1 = qT[:, dn:dn + hr]
    t2 = qT[:, dn + hr:dn + dr]
    r1 = (t1 * cos - t2 * sin) * qscale
    r2 = (t2 * cos + t1 * sin) * qscale
    qn = qT[:, 0:dn] * qscale
    nqb, kwq, tq = qT_ref.shape[2:]
    for j in range(nqb):
        cols = slice(j * tq, (j + 1) * tq)
        qT_ref[0, :, j, 0:dn] = qn[:, :, cols].astype(BF16)
        qT_ref[0, :, j, dn:dn + hr] = r1[:, :, cols].astype(BF16)
        qT_ref[0, :, j, dn + hr:dn + dr] = r2[:, :, cols].astype(BF16)
        qT_ref[0, :, j, dn + dr:] = jnp.zeros((H, kwq - dn - dr, tq), BF16)

    ckvb = ckvT.astype(BF16)
    vT = jnp.dot(wuv_ref[...], ckvb, preferred_element_type=F32)
    vT_ref[0] = vT.reshape(H, dv, tp).astype(BF16)

    kw = k_ref.shape[3]
    kpad = jnp.dot(ckvT.T.astype(BF16), wuk_ref[...], preferred_element_type=F32)
    k1 = krT[0:hr]
    k2 = krT[hr:dr]
    krT_pad = jnp.concatenate(
        [jnp.zeros((dn, tp), F32), k1 * cos - k2 * sin, k2 * cos + k1 * sin,
         jnp.zeros((kw - dn - dr, tp), F32)], axis=0)
    kr_pad = krT_pad.T
    for h in range(H):
        k_ref[0, h] = (kpad[:, h * kw:(h + 1) * kw] + kr_pad).astype(BF16)


def _mla_proj(x, cos, sin, w_dq, q_norm, w_uq, w_dkv, kv_norm, w_ukv, *, tq, tp=1024):
    B, S, D = x.shape
    H, dn, dr, dv = N_HEADS, QK_NOPE_DIM, QK_ROPE_DIM, V_HEAD_DIM
    hr = dr // 2
    ql = w_dq.shape[1]
    kvl = kv_norm.shape[0]
    kw = V7X_LANES
    wcatT = jnp.concatenate([w_dq, w_dkv], axis=1).T.astype(BF16)
    wuqT = w_uq.T.astype(BF16)
    w_ukv3 = w_ukv.reshape(kvl, H, dn + dv)
    wuvT = w_ukv3[:, :, dn:].reshape(kvl, H * dv).T.astype(BF16)
    wukp = jnp.pad(w_ukv3[:, :, :dn], ((0, 0), (0, 0), (0, kw - dn))).reshape(kvl, H * kw).astype(BF16)
    qscale = math.log2(math.e) / math.sqrt(dn + dr)
    kern = functools.partial(_mla_proj_kernel, ql=ql, kvl=kvl, qscale=qscale)
    return pl.pallas_call(
        kern,
        grid=(B, S // tp),
        in_specs=[
            pl.BlockSpec((1, tp, D), lambda b, i: (b, i, 0)),
            pl.BlockSpec((1, hr, tp), lambda b, i: (b, 0, i)),
            pl.BlockSpec((1, hr, tp), lambda b, i: (b, 0, i)),
            _const_spec(wcatT.shape), _const_spec((ql, 1)), _const_spec((kvl, 1)),
            _const_spec(wuqT.shape), _const_spec(wuvT.shape), _const_spec(wukp.shape),
        ],
        out_specs=[
            pl.BlockSpec((1, H, tp // tq, kw, tq), lambda b, i: (b, 0, i, 0, 0)),
            pl.BlockSpec((1, H, tp, kw), lambda b, i: (b, 0, i, 0)),
            pl.BlockSpec((1, H, dv, tp), lambda b, i: (b, 0, 0, i)),
        ],
        out_shape=[
            jax.ShapeDtypeStruct((B, H, S // tq, kw, tq), BF16),
            jax.ShapeDtypeStruct((B, H, S, kw), BF16),
            jax.ShapeDtypeStruct((B, H, dv, S), BF16),
        ],
        compiler_params=_cparams(2),
        name="mla_proj",
    )(x, cos, sin, wcatT, q_norm.reshape(ql, 1), kv_norm.reshape(kvl, 1), wuqT, wuvT, wukp)


def _attn_kernel(qT_ref, k_ref, vT_ref, o_ref, sa_ref, sb_ref, ma_ref, mb_ref, oT_ref, *, tk):
    hb, nqb, _, tq = qT_ref.shape[1:]
    S = k_ref.shape[2]
    dv = vT_ref.shape[2]
    nc = S // tk
    n_items = hb * nqb

    def stage(t1, s1_ref, m1_ref, t2, s2_ref, m2_ref):
        if t1 is not None:
            h1 = t1 // nqb
            qT = qT_ref[0, h1, t1 % nqb]
            mx = jnp.full((8, tq), -jnp.inf, F32)
        if t2 is not None:
            h2 = t2 // nqb
            m8 = jnp.broadcast_to(jnp.max(m2_ref[...], axis=0, keepdims=True), (8, tq))
            l8 = jnp.zeros((8, tq), F32)
            acc = jnp.zeros((dv, tq), F32)
        for c in range(nc):
            rows = slice(c * tk, (c + 1) * tk)
            if t1 is not None:
                s = jnp.dot(k_ref[0, h1, rows, :], qT, preferred_element_type=F32)
                s1_ref[rows, :] = s
                mx = jnp.maximum(mx, jnp.max(s.reshape(tk // 8, 8, tq), axis=0))
            if t2 is not None:
                p = jnp.exp2(s2_ref[rows, :].reshape(tk // 8, 8, tq) - m8)
                l8 = l8 + jnp.sum(p, axis=0)
                acc = acc + jnp.dot(vT_ref[0, h2, :, rows], p.reshape(tk, tq).astype(BF16),
                                    preferred_element_type=F32)
        if t1 is not None:
            m1_ref[...] = mx
        if t2 is not None:
            oT_ref[h2, t2 % nqb] = acc / jnp.sum(l8, axis=0, keepdims=True)

    stage(0, sa_ref, ma_ref, None, None, None)

    def pair(j, carry):
        stage(2 * j + 1, sb_ref, mb_ref, 2 * j, sa_ref, ma_ref)
        stage(2 * j + 2, sa_ref, ma_ref, 2 * j + 1, sb_ref, mb_ref)
        return carry

    lax.fori_loop(0, n_items // 2 - 1, pair, 0)
    stage(n_items - 1, sb_ref, mb_ref, n_items - 2, sa_ref, ma_ref)
    stage(None, None, None, n_items - 1, sb_ref, mb_ref)

    for qb in range(nqb):
        oT = jnp.concatenate([oT_ref[h, qb] for h in range(hb)], axis=0)
        o_ref[0, qb * tq:(qb + 1) * tq, :] = oT.T.astype(o_ref.dtype)


def _attention(qT, k, vT, *, tk=256, hb=2):
    B, H, nqb, kw, tq = qT.shape
    S = k.shape[2]
    dv = vT.shape[2]
    assert (hb * nqb) % 2 == 0 and hb * dv == V7X_LANES
    return pl.pallas_call(
        functools.partial(_attn_kernel, tk=tk),
        grid=(B, H // hb),
        in_specs=[
            pl.BlockSpec((1, hb, nqb, kw, tq), lambda b, h: (b, h, 0, 0, 0)),
            pl.BlockSpec((1, hb, S, kw), lambda b, h: (b, h, 0, 0)),
            pl.BlockSpec((1, hb, dv, S), lambda b, h: (b, h, 0, 0)),
        ],
        out_specs=pl.BlockSpec((1, S, hb * dv), lambda b, h: (b, 0, h)),
        out_shape=jax.ShapeDtypeStruct((B, S, H * dv), BF16),
        scratch_shapes=[
            pltpu.VMEM((S, tq), F32), pltpu.VMEM((S, tq), F32),
            pltpu.VMEM((8, tq), F32), pltpu.VMEM((8, tq), F32),
            pltpu.VMEM((hb, nqb, dv, tq), F32),
        ],
        compiler_params=_cparams(2),
        name="mla_attention",
    )(qT, k, vT)


def _proj_ln_kernel(x_ref, a_ref, w_ref, g_ref, b_ref, o_ref, *, alpha):
    h = jnp.dot(a_ref[...], w_ref[...], preferred_element_type=F32)
    o_ref[...] = _layer_norm(alpha * x_ref[...] + h, g_ref[...], b_ref[...])


def _proj_ln(x2, a2, w, g, b, alpha, *, tm=512):
    T, D = x2.shape
    K = a2.shape[1]
    return pl.pallas_call(
        functools.partial(_proj_ln_kernel, alpha=alpha),
        grid=(T // tm,),
        in_specs=[
            pl.BlockSpec((tm, D), lambda i: (i, 0)),
            pl.BlockSpec((tm, K), lambda i: (i, 0)),
            _const_spec((K, D)), _const_spec((1, D)), _const_spec((1, D)),
        ],
        out_specs=pl.BlockSpec((tm, D), lambda i: (i, 0)),
        out_shape=jax.ShapeDtypeStruct((T, D), F32),
        compiler_params=_cparams(1),
        name="out_proj_ln",
    )(x2, a2, w.astype(BF16), g.reshape(1, D), b.reshape(1, D))


def _ffn_kernel(x_ref, w1_ref, w3_ref, w2_ref, g_ref, b_ref, o_ref, gate_ref, *, alpha, fc):
    x = x_ref[...]
    xb = x.astype(BF16)
    dff = w1_ref.shape[1]
    for c in range(dff // fc):
        a = jnp.dot(xb, w1_ref[:, c * fc:(c + 1) * fc], preferred_element_type=F32)
        u = jnp.dot(xb, w3_ref[:, c * fc:(c + 1) * fc], preferred_element_type=F32)
        gate_ref[:, c * fc:(c + 1) * fc] = (a * _sigmoid(a) * u).astype(BF16)
    f = jnp.dot(gate_ref[...], w2_ref[...], preferred_element_type=F32)
    o_ref[...] = _layer_norm(alpha * x + f, g_ref[...], b_ref[...])


def _ffn_ln(x2, w1, w3, w2, g, b, alpha, *, tm=1024, fc=256):
    T, D = x2.shape
    dff = w1.shape[1]
    return pl.pallas_call(
        functools.partial(_ffn_kernel, alpha=alpha, fc=fc),
        grid=(T // tm,),
        in_specs=[
            pl.BlockSpec((tm, D), lambda i: (i, 0)),
            _const_spec((D, dff)), _const_spec((D, dff)), _const_spec((dff, D)),
            _const_spec((1, D)), _const_spec((1, D)),
        ],
        out_specs=pl.BlockSpec((tm, D), lambda i: (i, 0)),
        out_shape=jax.ShapeDtypeStruct((T, D), F32),
        scratch_shapes=[pltpu.VMEM((tm, dff), BF16)],
        compiler_params=_cparams(1),
        name="ffn_ln",
    )(x2, w1.astype(BF16), w3.astype(BF16), w2.astype(BF16), g.reshape(1, D), b.reshape(1, D))


def _glu_kernel(x_ref, w_ref, b_ref, o_ref):
    D = o_ref.shape[1]
    h = jnp.dot(x_ref[...].astype(BF16), w_ref[...], preferred_element_type=F32) + b_ref[...]
    o_ref[...] = h[:, :D] * _sigmoid(h[:, D:])


def _conv_glu(x2, w_in, b_in, *, tm=1024):
    T, D = x2.shape
    return pl.pallas_call(
        _glu_kernel,
        grid=(T // tm,),
        in_specs=[pl.BlockSpec((tm, D), lambda i: (i, 0)), _const_spec((D, 2 * D)), _const_spec((1, 2 * D))],
        out_specs=pl.BlockSpec((tm, D), lambda i: (i, 0)),
        out_shape=jax.ShapeDtypeStruct((T, D), F32),
        compiler_params=_cparams(1),
        name="conv_glu",
    )(x2, w_in.astype(BF16), b_in.reshape(1, 2 * D))


def _conv_mid_kernel(x_ref, h_ref, hp_ref, hn_ref, dw_ref, dwb_ref, lg_ref, lb_ref, wo_ref, bo_ref,
                     g_ref, b_ref, o_ref, buf_ref, cv_ref, *, alpha, width, rb):
    ts = h_ref.shape[1]
    nl = buf_ref.shape[0]
    L = V7X_LANES
    i = pl.program_id(1)
    n = pl.num_programs(1)
    prev_ok = (i > 0).astype(F32)
    next_ok = (i < n - 1).astype(F32)
    for lc in range(nl):
        buf_ref[lc, 0:HALO, :] = hp_ref[0, :, lc * L:(lc + 1) * L] * prev_ok
        buf_ref[lc, HALO:HALO + ts, :] = h_ref[0, :, lc * L:(lc + 1) * L]
        buf_ref[lc, HALO + ts:, :] = hn_ref[0, :, lc * L:(lc + 1) * L] * next_ok

    pad = width // 2
    off = HALO - pad

    def lane_chunk(lc, carry):
        for r0 in range(0, ts, rb):
            acc = jnp.zeros((rb, L), F32)
            for j in range(width):
                acc = acc + buf_ref[lc, r0 + off + j:r0 + off + j + rb, :] * dw_ref[lc, j:j + 1, :]
            cv_ref[lc, r0:r0 + rb, :] = acc + dwb_ref[lc]
        return carry

    lax.fori_loop(0, nl, lane_chunk, 0)

    cv = jnp.concatenate([cv_ref[lc] for lc in range(nl)], axis=1)
    y = _layer_norm(cv, lg_ref[...], lb_ref[...])
    y = y * _sigmoid(y)
    hout = jnp.dot(y.astype(BF16), wo_ref[...], preferred_element_type=F32) + bo_ref[...]
    o_ref[0] = _layer_norm(alpha * x_ref[0] + hout, g_ref[...], b_ref[...])


def _conv_mid(x, h, dw, dw_b, ln_g, ln_b, w_out, b_out, g, b, alpha, *, ts=512, rb=256):
    B, S, D = x.shape
    width = dw.shape[0]
    L = V7X_LANES
    nl = D // L
    assert width // 2 <= HALO and ts % HALO == 0 and ts % rb == 0
    hblk = ts // HALO
    dw3 = jnp.pad(dw, ((0, 2 * HALO - width), (0, 0))).reshape(2 * HALO, nl, L).transpose(1, 0, 2)
    dwb3 = dw_b.reshape(nl, 1, L)
    row = lambda v: v.reshape(1, D)
    kern = functools.partial(_conv_mid_kernel, alpha=alpha, width=width, rb=rb)
    return pl.pallas_call(
        kern,
        grid=(B, S // ts),
        in_specs=[
            pl.BlockSpec((1, ts, D), lambda bb, i: (bb, i, 0)),
            pl.BlockSpec((1, ts, D), lambda bb, i: (bb, i, 0)),
            pl.BlockSpec((1, HALO, D), lambda bb, i: (bb, jnp.maximum(i * hblk - 1, 0), 0)),
            pl.BlockSpec((1, HALO, D), lambda bb, i: (bb, jnp.minimum((i + 1) * hblk, S // HALO - 1), 0)),
            _const_spec(dw3.shape), _const_spec(dwb3.shape),
            _const_spec((1, D)), _const_spec((1, D)),
            _const_spec((D, D)), _const_spec((1, D)),
            _const_spec((1, D)), _const_spec((1, D)),
        ],
        out_specs=pl.BlockSpec((1, ts, D), lambda bb, i: (bb, i, 0)),
        out_shape=jax.ShapeDtypeStruct((B, S, D), F32),
        scratch_shapes=[pltpu.VMEM((nl, ts + 2 * HALO, L), F32), pltpu.VMEM((nl, ts, L), F32)],
        compiler_params=_cparams(2),
        name="conv_mid",
    )(x, h, h, h, dw3, dwb3, row(ln_g), row(ln_b), w_out.astype(BF16), row(b_out), row(g), row(b))


def kernel(x, positions, ln_g, ln_b, mla_w_dq, mla_q_norm, mla_w_uq, mla_w_dkv, mla_kv_norm, mla_w_ukv, mla_w_o, conv_w_in, conv_b_in, conv_dw, conv_dw_b, conv_ln_g, conv_ln_b, conv_w_out, conv_b_out, ffn_w1, ffn_w3, ffn_w2):
    B, S, D = x.shape
    depth = ln_g.shape[0]
    alpha = float((2 * depth) ** 0.25)
    cos, sin = _rope_tables(positions)
    for i in range(depth):
        j = i // 2
        if i % 2 == 0:
            qT, k, vT = _mla_proj(x, cos, sin, mla_w_dq[j], mla_q_norm[j], mla_w_uq[j],
                                  mla_w_dkv[j], mla_kv_norm[j], mla_w_ukv[j], tq=512)
            o = _attention(qT, k, vT)
            x2 = _proj_ln(x.reshape(B * S, D), o.reshape(B * S, -1), mla_w_o[j],
                          ln_g[i, 0], ln_b[i, 0], alpha)
        else:
            h = _conv_glu(x.reshape(B * S, D), conv_w_in[j], conv_b_in[j]).reshape(B, S, D)
            x2 = _conv_mid(x, h, conv_dw[j], conv_dw_b[j], conv_ln_g[j], conv_ln_b[j],
                           conv_w_out[j], conv_b_out[j], ln_g[i, 0], ln_b[i, 0], alpha).reshape(B * S, D)
        x2 = _ffn_ln(x2, ffn_w1[i], ffn_w3[i], ffn_w2[i], ln_g[i, 1], ln_b[i, 1], alpha)
        x = x2.reshape(B, S, D)
    return x
```

```python
import functools
import math

import jax
import jax.numpy as jnp
from jax import lax
from jax.experimental import pallas as pl
from jax.experimental.pallas import tpu as pltpu

F32 = jnp.float32
BF16 = jnp.bfloat16

N_HEADS = 16
QK_NOPE_DIM = 64
QK_ROPE_DIM = 32
V_HEAD_DIM = 64
ROPE_THETA = 10000.0
LN_EPS = 1e-5
RMS_EPS = 1e-6

V7X_LANES = 128
V7X_VMEM_BYTES = 64 * 1024 * 1024
VMEM_LIMIT = (V7X_VMEM_BYTES * 3) // 4

HALO = 16


def _cparams(n_axes):
    return pltpu.CompilerParams(
        dimension_semantics=("arbitrary",) * n_axes, vmem_limit_bytes=VMEM_LIMIT)


def _const_spec(shape):
    nd = len(shape)
    return pl.BlockSpec(shape, lambda *_: (0,) * nd, pipeline_mode=pl.Buffered(1))


def _layer_norm(z, g, b):
    mu = jnp.mean(z, axis=-1, keepdims=True)
    d = z - mu
    var = jnp.mean(d * d, axis=-1, keepdims=True)
    return d * lax.rsqrt(var + LN_EPS) * g + b


def _sigmoid(a):
    return 1.0 / (1.0 + jnp.exp(-a))


def _rope_kernel(pos_ref, inv_ref, cos_ref, sin_ref):
    ang = inv_ref[...] * pos_ref[0].astype(F32)
    cos_ref[0] = jnp.cos(ang)
    sin_ref[0] = jnp.sin(ang)


def _rope_tables(positions):
    B, S = positions.shape
    hr = QK_ROPE_DIM // 2
    inv_freq = ROPE_THETA ** (-jnp.arange(0, QK_ROPE_DIM, 2, dtype=F32) / QK_ROPE_DIM)
    return pl.pallas_call(
        _rope_kernel,
        grid=(B,),
        in_specs=[pl.BlockSpec((1, 1, S), lambda b: (b, 0, 0)), _const_spec((hr, 1))],
        out_specs=[pl.BlockSpec((1, hr, S), lambda b: (b, 0, 0))] * 2,
        out_shape=[jax.ShapeDtypeStruct((B, hr, S), F32)] * 2,
        compiler_params=_cparams(1),
        name="rope_tables",
    )(positions.reshape(B, 1, S), inv_freq.reshape(hr, 1))


def _mla_proj_kernel(x_ref, cos_ref, sin_ref, wcat_ref, qn_ref, kvn_ref, wuq_ref, wuv_ref, wuk_ref,
                     qT_ref, k_ref, vT_ref, *, ql, kvl, qscale):
    H, dn, dr, dv = N_HEADS, QK_NOPE_DIM, QK_ROPE_DIM, V_HEAD_DIM
    hr = dr // 2
    tp = x_ref.shape[1]
    xb = x_ref[0].astype(BF16)
    zT = lax.dot_general(wcat_ref[...], xb, (((1,), (1,)), ((), ())),
                         preferred_element_type=F32)
    cqT = zT[0:ql]
    ckvT = zT[ql:ql + kvl]
    krT = zT[ql + kvl:ql + kvl + dr]
    cqT = cqT * lax.rsqrt(jnp.mean(cqT * cqT, axis=0, keepdims=True) + RMS_EPS) * qn_ref[...]
    ckvT = ckvT * lax.rsqrt(jnp.mean(ckvT * ckvT, axis=0, keepdims=True) + RMS_EPS) * kvn_ref[...]
    cos = cos_ref[0]
    sin = sin_ref[0]

    qT = jnp.dot(wuq_ref[...], cqT.astype(BF16), preferred_element_type=F32)
    qT = qT.reshape(H, dn + dr, tp)
    t1 = qT[:, dn:dn + hr]
    t2 = qT[:, dn + hr:dn + dr]
    r1 = (t1 * cos - t2 * sin) * qscale
    r2 = (t2 * cos + t1 * sin) * qscale
    qn = qT[:, 0:dn] * qscale
    nqb, kwq, tq = qT_ref.shape[2:]
    for j in range(nqb):
        cols = slice(j * tq, (j + 1) * tq)
        qT_ref[0, :, j, 0:dn] = qn[:, :, cols].astype(BF16)
        qT_ref[0, :, j, dn:dn + hr] = r1[:, :, cols].astype(BF16)
        qT_ref[0, :, j, dn + hr:dn + dr] = r2[:, :, cols].astype(BF16)
        qT_ref[0, :, j, dn + dr:] = jnp.zeros((H, kwq - dn - dr, tq), BF16)

    ckvb = ckvT.astype(BF16)
    vT = jnp.dot(wuv_ref[...], ckvb, preferred_element_type=F32)
    vT_ref[0] = vT.reshape(H, dv, tp).astype(BF16)

    kw = k_ref.shape[3]
    kpad = jnp.dot(ckvT.T.astype(BF16), wuk_ref[...], preferred_element_type=F32)
    k1 = krT[0:hr]
    k2 = krT[hr:dr]
    krT_pad = jnp.concatenate(
        [jnp.zeros((dn, tp), F32), k1 * cos - k2 * sin, k2 * cos + k1 * sin,
         jnp.zeros((kw - dn - dr, tp), F32)], axis=0)
    kr_pad = krT_pad.T
    for h in range(H):
        k_ref[0, h] = (kpad[:, h * kw:(h + 1) * kw] + kr_pad).astype(BF16)


def _mla_proj(x, cos, sin, w_dq, q_norm, w_uq, w_dkv, kv_norm, w_ukv, *, tq, tp=1024):
    B, S, D = x.shape
    H, dn, dr, dv = N_HEADS, QK_NOPE_DIM, QK_ROPE_DIM, V_HEAD_DIM
    hr = dr // 2
    ql = w_dq.shape[1]
    kvl = kv_norm.shape[0]
    kw = V7X_LANES
    wcatT = jnp.concatenate([w_dq, w_dkv], axis=1).T.astype(BF16)
    wuqT = w_uq.T.astype(BF16)
    w_ukv3 = w_ukv.reshape(kvl, H, dn + dv)
    wuvT = w_ukv3[:, :, dn:].reshape(kvl, H * dv).T.astype(BF16)
    wukp = jnp.pad(w_ukv3[:, :, :dn], ((0, 0), (0, 0), (0, kw - dn))).reshape(kvl, H * kw).astype(BF16)
    qscale = math.log2(math.e) / math.sqrt(dn + dr)
    kern = functools.partial(_mla_proj_kernel, ql=ql, kvl=kvl, qscale=qscale)
    return pl.pallas_call(
        kern,
        grid=(B, S // tp),
        in_specs=[
            pl.BlockSpec((1, tp, D), lambda b, i: (b, i, 0)),
            pl.BlockSpec((1, hr, tp), lambda b, i: (b, 0, i)),
            pl.BlockSpec((1, hr, tp), lambda b, i: (b, 0, i)),
            _const_spec(wcatT.shape), _const_spec((ql, 1)), _const_spec((kvl, 1)),
            _const_spec(wuqT.shape), _const_spec(wuvT.shape), _const_spec(wukp.shape),
        ],
        out_specs=[
            pl.BlockSpec((1, H, tp // tq, kw, tq), lambda b, i: (b, 0, i, 0, 0)),
            pl.BlockSpec((1, H, tp, kw), lambda b, i: (b, 0, i, 0)),
            pl.BlockSpec((1, H, dv, tp), lambda b, i: (b, 0, 0, i)),
        ],
        out_shape=[
            jax.ShapeDtypeStruct((B, H, S // tq, kw, tq), BF16),
            jax.ShapeDtypeStruct((B, H, S, kw), BF16),
            jax.ShapeDtypeStruct((B, H, dv, S), BF16),
        ],
        compiler_params=_cparams(2),
        name="mla_proj",
    )(x, cos, sin, wcatT, q_norm.reshape(ql, 1), kv_norm.reshape(kvl, 1), wuqT, wuvT, wukp)


def _attn_kernel(qT_ref, k_ref, vT_ref, o_ref, sa_ref, sb_ref, ma_ref, mb_ref, oT_ref, *, tk):
    hb, nqb, _, tq = qT_ref.shape[1:]
    S = k_ref.shape[2]
    dv = vT_ref.shape[2]
    nc = S // tk
    n_items = hb * nqb

    def stage(t1, s1_ref, m1_ref, t2, s2_ref, m2_ref):
        if t1 is not None:
            h1 = t1 // nqb
            qT = qT_ref[0, h1, t1 % nqb]
            mx = jnp.full((8, tq), -jnp.inf, F32)
        if t2 is not None:
            h2 = t2 // nqb
            m8 = jnp.broadcast_to(jnp.max(m2_ref[...], axis=0, keepdims=True), (8, tq))
            l8 = jnp.zeros((8, tq), F32)
            acc = jnp.zeros((dv, tq), F32)
        for c in range(nc):
            rows = slice(c * tk, (c + 1) * tk)
            if t1 is not None:
                s = jnp.dot(k_ref[0, h1, rows, :], qT, preferred_element_type=F32)
                s1_ref[rows, :] = s
                mx = jnp.maximum(mx, jnp.max(s.reshape(tk // 8, 8, tq), axis=0))
            if t2 is not None:
                p = jnp.exp2(s2_ref[rows, :].reshape(tk // 8, 8, tq) - m8)
                l8 = l8 + jnp.sum(p, axis=0)
                acc = acc + jnp.dot(vT_ref[0, h2, :, rows], p.reshape(tk, tq).astype(BF16),
                                    preferred_element_type=F32)
        if t1 is not None:
            m1_ref[...] = mx
        if t2 is not None:
            oT_ref[h2, t2 % nqb] = acc / jnp.sum(l8, axis=0, keepdims=True)

    stage(0, sa_ref, ma_ref, None, None, None)

    def pair(j, carry):
        stage(2 * j + 1, sb_ref, mb_ref, 2 * j, sa_ref, ma_ref)
        stage(2 * j + 2, sa_ref, ma_ref, 2 * j + 1, sb_ref, mb_ref)
        return carry

    lax.fori_loop(0, n_items // 2 - 1, pair, 0)
    stage(n_items - 1, sb_ref, mb_ref, n_items - 2, sa_ref, ma_ref)
    stage(None, None, None, n_items - 1, sb_ref, mb_ref)

    for qb in range(nqb):
        oT = jnp.concatenate([oT_ref[h, qb] for h in range(hb)], axis=0)
        o_ref[0, qb * tq:(qb + 1) * tq, :] = oT.T.astype(o_ref.dtype)


def _attention(qT, k, vT, *, tk=256, hb=2):
    B, H, nqb, kw, tq = qT.shape
    S = k.shape[2]
    dv = vT.shape[2]
    assert (hb * nqb) % 2 == 0 and hb * dv == V7X_LANES
    return pl.pallas_call(
        functools.partial(_attn_kernel, tk=tk),
        grid=(B, H // hb),
        in_specs=[
            pl.BlockSpec((1, hb, nqb, kw, tq), lambda b, h: (b, h, 0, 0, 0)),
            pl.BlockSpec((1, hb, S, kw), lambda b, h: (b, h, 0, 0)),
            pl.BlockSpec((1, hb, dv, S), lambda b, h: (b, h, 0, 0)),
        ],
        out_specs=pl.BlockSpec((1, S, hb * dv), lambda b, h: (b, 0, h)),
        out_shape=jax.ShapeDtypeStruct((B, S, H * dv), BF16),
        scratch_shapes=[
            pltpu.VMEM((S, tq), F32), pltpu.VMEM((S, tq), F32),
            pltpu.VMEM((8, tq), F32), pltpu.VMEM((8, tq), F32),
            pltpu.VMEM((hb, nqb, dv, tq), F32),
        ],
        compiler_params=_cparams(2),
        name="mla_attention",
    )(qT, k, vT)


def _swiglu_ln(x1, w1_ref, w3_ref, w2_ref, g, b, gate_ref, rows, *, alpha, fc):
    xb = x1.astype(BF16)
    dff = w1_ref.shape[1]
    for c in range(dff // fc):
        cols = slice(c * fc, (c + 1) * fc)
        a = jnp.dot(xb, w1_ref[:, cols], preferred_element_type=F32)
        u = jnp.dot(xb, w3_ref[:, cols], preferred_element_type=F32)
        gate_ref[rows, cols] = (a * _sigmoid(a) * u).astype(BF16)
    f = jnp.dot(gate_ref[rows, :], w2_ref[...], preferred_element_type=F32)
    return _layer_norm(alpha * x1 + f, g, b)


def _proj_ffn_kernel(x_ref, a_ref, wo_ref, g1_ref, b1_ref, w1_ref, w3_ref, w2_ref, g2_ref, b2_ref,
                     o_ref, gate_ref, *, alpha, fc, sub):
    for r0 in range(0, x_ref.shape[0], sub):
        rows = slice(r0, r0 + sub)
        h = jnp.dot(a_ref[rows, :], wo_ref[...], preferred_element_type=F32)
        x1 = _layer_norm(alpha * x_ref[rows, :] + h, g1_ref[...], b1_ref[...])
        o_ref[rows, :] = _swiglu_ln(x1, w1_ref, w3_ref, w2_ref, g2_ref[...], b2_ref[...],
                                    gate_ref, rows, alpha=alpha, fc=fc)


def _proj_ffn_ln(x2, a2, wo, g1, b1, w1, w3, w2, g2, b2, alpha, *, tm=1024, fc=256, sub=256):
    T, D = x2.shape
    K = a2.shape[1]
    dff = w1.shape[1]
    assert dff % fc == 0 and tm % sub == 0
    row = lambda v: v.reshape(1, D)
    return pl.pallas_call(
        functools.partial(_proj_ffn_kernel, alpha=alpha, fc=fc, sub=sub),
        grid=(T // tm,),
        in_specs=[
            pl.BlockSpec((tm, D), lambda i: (i, 0)),
            pl.BlockSpec((tm, K), lambda i: (i, 0)),
            _const_spec((K, D)), _const_spec((1, D)), _const_spec((1, D)),
            _const_spec((D, dff)), _const_spec((D, dff)), _const_spec((dff, D)),
            _const_spec((1, D)), _const_spec((1, D)),
        ],
        out_specs=pl.BlockSpec((tm, D), lambda i: (i, 0)),
        out_shape=jax.ShapeDtypeStruct((T, D), F32),
        scratch_shapes=[pltpu.VMEM((tm, dff), BF16)],
        compiler_params=_cparams(1),
        name="proj_ffn_ln",
    )(x2, a2, wo.astype(BF16), row(g1), row(b1), w1.astype(BF16), w3.astype(BF16), w2.astype(BF16),
      row(g2), row(b2))


def _ffn_kernel(x_ref, w1_ref, w3_ref, w2_ref, g_ref, b_ref, o_ref, gate_ref, *, alpha, fc):
    x = x_ref[...]
    xb = x.astype(BF16)
    dff = w1_ref.shape[1]
    for c in range(dff // fc):
        a = jnp.dot(xb, w1_ref[:, c * fc:(c + 1) * fc], preferred_element_type=F32)
        u = jnp.dot(xb, w3_ref[:, c * fc:(c + 1) * fc], preferred_element_type=F32)
        gate_ref[:, c * fc:(c + 1) * fc] = (a * _sigmoid(a) * u).astype(BF16)
    f = jnp.dot(gate_ref[...], w2_ref[...], preferred_element_type=F32)
    o_ref[...] = _layer_norm(alpha * x + f, g_ref[...], b_ref[...])


def _ffn_ln(x2, w1, w3, w2, g, b, alpha, *, tm=1024, fc=256):
    T, D = x2.shape
    dff = w1.shape[1]
    return pl.pallas_call(
        functools.partial(_ffn_kernel, alpha=alpha, fc=fc),
        grid=(T // tm,),
        in_specs=[
            pl.BlockSpec((tm, D), lambda i: (i, 0)),
            _const_spec((D, dff)), _const_spec((D, dff)), _const_spec((dff, D)),
            _const_spec((1, D)), _const_spec((1, D)),
        ],
        out_specs=pl.BlockSpec((tm, D), lambda i: (i, 0)),
        out_shape=jax.ShapeDtypeStruct((T, D), F32),
        scratch_shapes=[pltpu.VMEM((tm, dff), BF16)],
        compiler_params=_cparams(1),
        name="ffn_ln",
    )(x2, w1.astype(BF16), w3.astype(BF16), w2.astype(BF16), g.reshape(1, D), b.reshape(1, D))


def _glu_kernel(x_ref, w_ref, b_ref, o_ref):
    D = o_ref.shape[1]
    h = jnp.dot(x_ref[...].astype(BF16), w_ref[...], preferred_element_type=F32) + b_ref[...]
    o_ref[...] = h[:, :D] * _sigmoid(h[:, D:])


def _conv_glu(x2, w_in, b_in, *, tm=1024):
    T, D = x2.shape
    return pl.pallas_call(
        _glu_kernel,
        grid=(T // tm,),
        in_specs=[pl.BlockSpec((tm, D), lambda i: (i, 0)), _const_spec((D, 2 * D)), _const_spec((1, 2 * D))],
        out_specs=pl.BlockSpec((tm, D), lambda i: (i, 0)),
        out_shape=jax.ShapeDtypeStruct((T, D), F32),
        compiler_params=_cparams(1),
        name="conv_glu",
    )(x2, w_in.astype(BF16), b_in.reshape(1, 2 * D))


def _conv_mid_kernel(x_ref, h_ref, hp_ref, hn_ref, dw_ref, dwb_ref, lg_ref, lb_ref, wo_ref, bo_ref,
                     g_ref, b_ref, o_ref, buf_ref, cv_ref, *, alpha, width, rb):
    ts = h_ref.shape[1]
    nl = buf_ref.shape[0]
    L = V7X_LANES
    i = pl.program_id(1)
    n = pl.num_programs(1)
    prev_ok = (i > 0).astype(F32)
    next_ok = (i < n - 1).astype(F32)
    for lc in range(nl):
        buf_ref[lc, 0:HALO, :] = hp_ref[0, :, lc * L:(lc + 1) * L] * prev_ok
        buf_ref[lc, HALO:HALO + ts, :] = h_ref[0, :, lc * L:(lc + 1) * L]
        buf_ref[lc, HALO + ts:, :] = hn_ref[0, :, lc * L:(lc + 1) * L] * next_ok

    pad = width // 2
    off = HALO - pad

    def lane_chunk(lc, carry):
        for r0 in range(0, ts, rb):
            acc = jnp.zeros((rb, L), F32)
            for j in range(width):
                acc = acc + buf_ref[lc, r0 + off + j:r0 + off + j + rb, :] * dw_ref[lc, j:j + 1, :]
            cv_ref[lc, r0:r0 + rb, :] = acc + dwb_ref[lc]
        return carry

    lax.fori_loop(0, nl, lane_chunk, 0)

    cv = jnp.concatenate([cv_ref[lc] for lc in range(nl)], axis=1)
    y = _layer_norm(cv, lg_ref[...], lb_ref[...])
    y = y * _sigmoid(y)
    hout = jnp.dot(y.astype(BF16), wo_ref[...], preferred_element_type=F32) + bo_ref[...]
    o_ref[0] = _layer_norm(alpha * x_ref[0] + hout, g_ref[...], b_ref[...])


def _conv_mid(x, h, dw, dw_b, ln_g, ln_b, w_out, b_out, g, b, alpha, *, ts=512, rb=256):
    B, S, D = x.shape
    width = dw.shape[0]
    L = V7X_LANES
    nl = D // L
    assert width // 2 <= HALO and ts % HALO == 0 and ts % rb == 0
    hblk = ts // HALO
    dw3 = jnp.pad(dw, ((0, 2 * HALO - width), (0, 0))).reshape(2 * HALO, nl, L).transpose(1, 0, 2)
    dwb3 = dw_b.reshape(nl, 1, L)
    row = lambda v: v.reshape(1, D)
    kern = functools.partial(_conv_mid_kernel, alpha=alpha, width=width, rb=rb)
    return pl.pallas_call(
        kern,
        grid=(B, S // ts),
        in_specs=[
            pl.BlockSpec((1, ts, D), lambda bb, i: (bb, i, 0)),
            pl.BlockSpec((1, ts, D), lambda bb, i: (bb, i, 0)),
            pl.BlockSpec((1, HALO, D), lambda bb, i: (bb, jnp.maximum(i * hblk - 1, 0), 0)),
            pl.BlockSpec((1, HALO, D), lambda bb, i: (bb, jnp.minimum((i + 1) * hblk, S // HALO - 1), 0)),
            _const_spec(dw3.shape), _const_spec(dwb3.shape),
            _const_spec((1, D)), _const_spec((1, D)),
            _const_spec((D, D)), _const_spec((1, D)),
            _const_spec((1, D)), _const_spec((1, D)),
        ],
        out_specs=pl.BlockSpec((1, ts, D), lambda bb, i: (bb, i, 0)),
        out_shape=jax.ShapeDtypeStruct((B, S, D), F32),
        scratch_shapes=[pltpu.VMEM((nl, ts + 2 * HALO, L), F32), pltpu.VMEM((nl, ts, L), F32)],
        compiler_params=_cparams(2),
        name="conv_mid",
    )(x, h, h, h, dw3, dwb3, row(ln_g), row(ln_b), w_out.astype(BF16), row(b_out), row(g), row(b))


def kernel(x, positions, ln_g, ln_b, mla_w_dq, mla_q_norm, mla_w_uq, mla_w_dkv, mla_kv_norm, mla_w_ukv, mla_w_o, conv_w_in, conv_b_in, conv_dw, conv_dw_b, conv_ln_g, conv_ln_b, conv_w_out, conv_b_out, ffn_w1, ffn_w3, ffn_w2):
    B, S, D = x.shape
    depth = ln_g.shape[0]
    alpha = float((2 * depth) ** 0.25)
    cos, sin = _rope_tables(positions)
    for i in range(depth):
        j = i // 2
        ffn = (ffn_w1[i], ffn_w3[i], ffn_w2[i], ln_g[i, 1], ln_b[i, 1], alpha)
        if i % 2 == 0:
            qT, k, vT = _mla_proj(x, cos, sin, mla_w_dq[j], mla_q_norm[j], mla_w_uq[j],
                                  mla_w_dkv[j], mla_kv_norm[j], mla_w_ukv[j], tq=512)
            o = _attention(qT, k, vT)
            x2 = _proj_ffn_ln(x.reshape(B * S, D), o.reshape(B * S, -1), mla_w_o[j],
                              ln_g[i, 0], ln_b[i, 0], *ffn)
        else:
            h = _conv_glu(x.reshape(B * S, D), conv_w_in[j], conv_b_in[j]).reshape(B, S, D)
            x1 = _conv_mid(x, h, conv_dw[j], conv_dw_b[j], conv_ln_g[j], conv_ln_b[j],
                           conv_w_out[j], conv_b_out[j], ln_g[i, 0], ln_b[i, 0], alpha)
            x2 = _ffn_ln(x1.reshape(B * S, D), *ffn)
        x = x2.reshape(B, S, D)
    return x
```

```python
import functools
import math

import jax
import jax.numpy as jnp
from jax import lax
from jax.experimental import pallas as pl
from jax.experimental.pallas import tpu as pltpu

F32 = jnp.float32
BF16 = jnp.bfloat16

N_HEADS = 16
QK_NOPE_DIM = 64
QK_ROPE_DIM = 32
V_HEAD_DIM = 64
ROPE_THETA = 10000.0
LN_EPS = 1e-5
RMS_EPS = 1e-6

V7X_LANES = 128
V7X_VMEM_BYTES = 64 * 1024 * 1024
VMEM_LIMIT = (V7X_VMEM_BYTES * 3) // 4

HALO = 16


def _cparams(n_axes):
    return pltpu.CompilerParams(
        dimension_semantics=("arbitrary",) * n_axes, vmem_limit_bytes=VMEM_LIMIT)


def _const_spec(shape):
    nd = len(shape)
    return pl.BlockSpec(shape, lambda *_: (0,) * nd, pipeline_mode=pl.Buffered(1))


def _layer_norm(z, g, b):
    mu = jnp.mean(z, axis=-1, keepdims=True)
    d = z - mu
    var = jnp.mean(d * d, axis=-1, keepdims=True)
    return d * lax.rsqrt(var + LN_EPS) * g + b


def _sigmoid(a):
    return 1.0 / (1.0 + jnp.exp(-a))


def _rope_kernel(pos_ref, inv_ref, cos_ref, sin_ref):
    ang = inv_ref[...] * pos_ref[0].astype(F32)
    cos_ref[0] = jnp.cos(ang)
    sin_ref[0] = jnp.sin(ang)


def _rope_tables(positions):
    B, S = positions.shape
    hr = QK_ROPE_DIM // 2
    inv_freq = ROPE_THETA ** (-jnp.arange(0, QK_ROPE_DIM, 2, dtype=F32) / QK_ROPE_DIM)
    return pl.pallas_call(
        _rope_kernel,
        grid=(B,),
        in_specs=[pl.BlockSpec((1, 1, S), lambda b: (b, 0, 0)), _const_spec((hr, 1))],
        out_specs=[pl.BlockSpec((1, hr, S), lambda b: (b, 0, 0))] * 2,
        out_shape=[jax.ShapeDtypeStruct((B, hr, S), F32)] * 2,
        compiler_params=_cparams(1),
        name="rope_tables",
    )(positions.reshape(B, 1, S), inv_freq.reshape(hr, 1))


def _mla_proj_kernel(x_ref, cos_ref, sin_ref, wcat_ref, qn_ref, kvn_ref, wuq_ref, wuv_ref, wuk_ref,
                     qT_ref, k_ref, vT_ref, *, ql, kvl, qscale):
    H, dn, dr, dv = N_HEADS, QK_NOPE_DIM, QK_ROPE_DIM, V_HEAD_DIM
    hr = dr // 2
    tp = x_ref.shape[1]
    xb = x_ref[0].astype(BF16)
    zT = lax.dot_general(wcat_ref[...], xb, (((1,), (1,)), ((), ())),
                         preferred_element_type=F32)
    cqT = zT[0:ql]
    ckvT = zT[ql:ql + kvl]
    krT = zT[ql + kvl:ql + kvl + dr]
    cqT = cqT * lax.rsqrt(jnp.mean(cqT * cqT, axis=0, keepdims=True) + RMS_EPS) * qn_ref[...]
    ckvT = ckvT * lax.rsqrt(jnp.mean(ckvT * ckvT, axis=0, keepdims=True) + RMS_EPS) * kvn_ref[...]
    cos = cos_ref[0]
    sin = sin_ref[0]

    qT = jnp.dot(wuq_ref[...], cqT.astype(BF16), preferred_element_type=F32)
    qT = qT.reshape(H, dn + dr, tp)
    t1 = qT[:, dn:dn + hr]
    t2 = qT[:, dn + hr:dn + dr]
    r1 = (t1 * cos - t2 * sin) * qscale
    r2 = (t2 * cos + t1 * sin) * qscale
    qn = qT[:, 0:dn] * qscale
    nqb, kwq, tq = qT_ref.shape[2:]
    for j in range(nqb):
        cols = slice(j * tq, (j + 1) * tq)
        qT_ref[0, :, j, 0:dn] = qn[:, :, cols].astype(BF16)
        qT_ref[0, :, j, dn:dn + hr] = r1[:, :, cols].astype(BF16)
        qT_ref[0, :, j, dn + hr:dn + dr] = r2[:, :, cols].astype(BF16)
        qT_ref[0, :, j, dn + dr:] = jnp.zeros((H, kwq - dn - dr, tq), BF16)

    ckvb = ckvT.astype(BF16)
    vT = jnp.dot(wuv_ref[...], ckvb, preferred_element_type=F32)
    vT_ref[0] = vT.reshape(H, dv, tp).astype(BF16)

    kw = k_ref.shape[3]
    kpad = jnp.dot(ckvT.T.astype(BF16), wuk_ref[...], preferred_element_type=F32)
    k1 = krT[0:hr]
    k2 = krT[hr:dr]
    krT_pad = jnp.concatenate(
        [jnp.zeros((dn, tp), F32), k1 * cos - k2 * sin, k2 * cos + k1 * sin,
         jnp.zeros((kw - dn - dr, tp), F32)], axis=0)
    kr_pad = krT_pad.T
    for h in range(H):
        k_ref[0, h] = (kpad[:, h * kw:(h + 1) * kw] + kr_pad).astype(BF16)


def _mla_proj(x, cos, sin, w_dq, q_norm, w_uq, w_dkv, kv_norm, w_ukv, *, tq, tp=1024):
    B, S, D = x.shape
    H, dn, dr, dv = N_HEADS, QK_NOPE_DIM, QK_ROPE_DIM, V_HEAD_DIM
    hr = dr // 2
    ql = w_dq.shape[1]
    kvl = kv_norm.shape[0]
    kw = V7X_LANES
    wcatT = jnp.concatenate([w_dq, w_dkv], axis=1).T.astype(BF16)
    wuqT = w_uq.T.astype(BF16)
    w_ukv3 = w_ukv.reshape(kvl, H, dn + dv)
    wuvT = w_ukv3[:, :, dn:].reshape(kvl, H * dv).T.astype(BF16)
    wukp = jnp.pad(w_ukv3[:, :, :dn], ((0, 0), (0, 0), (0, kw - dn))).reshape(kvl, H * kw).astype(BF16)
    qscale = math.log2(math.e) / math.sqrt(dn + dr)
    kern = functools.partial(_mla_proj_kernel, ql=ql, kvl=kvl, qscale=qscale)
    return pl.pallas_call(
        kern,
        grid=(B, S // tp),
        in_specs=[
            pl.BlockSpec((1, tp, D), lambda b, i: (b, i, 0)),
            pl.BlockSpec((1, hr, tp), lambda b, i: (b, 0, i)),
            pl.BlockSpec((1, hr, tp), lambda b, i: (b, 0, i)),
            _const_spec(wcatT.shape), _const_spec((ql, 1)), _const_spec((kvl, 1)),
            _const_spec(wuqT.shape), _const_spec(wuvT.shape), _const_spec(wukp.shape),
        ],
        out_specs=[
            pl.BlockSpec((1, H, tp // tq, kw, tq), lambda b, i: (b, 0, i, 0, 0)),
            pl.BlockSpec((1, H, tp, kw), lambda b, i: (b, 0, i, 0)),
            pl.BlockSpec((1, H, dv, tp), lambda b, i: (b, 0, 0, i)),
        ],
        out_shape=[
            jax.ShapeDtypeStruct((B, H, S // tq, kw, tq), BF16),
            jax.ShapeDtypeStruct((B, H, S, kw), BF16),
            jax.ShapeDtypeStruct((B, H, dv, S), BF16),
        ],
        compiler_params=_cparams(2),
        name="mla_proj",
    )(x, cos, sin, wcatT, q_norm.reshape(ql, 1), kv_norm.reshape(kvl, 1), wuqT, wuvT, wukp)


def _attn_kernel(qT_ref, k_ref, vT_ref, o_ref, sa_ref, sb_ref, ma_ref, mb_ref, oT_ref, *, tk):
    hb, nqb, _, tq = qT_ref.shape[1:]
    S = k_ref.shape[2]
    dv = vT_ref.shape[2]
    nc = S // tk
    n_items = hb * nqb

    def stage(t1, s1_ref, m1_ref, t2, s2_ref, m2_ref):
        if t1 is not None:
            h1 = t1 // nqb
            qT = qT_ref[0, h1, t1 % nqb]
            mx = jnp.full((8, tq), -jnp.inf, F32)
        if t2 is not None:
            h2 = t2 // nqb
            m8 = jnp.broadcast_to(jnp.max(m2_ref[...], axis=0, keepdims=True), (8, tq))
            l8 = jnp.zeros((8, tq), F32)
            acc = jnp.zeros((dv, tq), F32)
        for c in range(nc):
            rows = slice(c * tk, (c + 1) * tk)
            if t1 is not None:
                s = jnp.dot(k_ref[0, h1, rows, :], qT, preferred_element_type=F32)
                s1_ref[rows, :] = s
                mx = jnp.maximum(mx, jnp.max(s.reshape(tk // 8, 8, tq), axis=0))
            if t2 is not None:
                p = jnp.exp2(s2_ref[rows, :].reshape(tk // 8, 8, tq) - m8)
                l8 = l8 + jnp.sum(p, axis=0)
                acc = acc + jnp.dot(vT_ref[0, h2, :, rows], p.reshape(tk, tq).astype(BF16),
                                    preferred_element_type=F32)
        if t1 is not None:
            m1_ref[...] = mx
        if t2 is not None:
            oT_ref[h2, t2 % nqb] = acc / jnp.sum(l8, axis=0, keepdims=True)

    stage(0, sa_ref, ma_ref, None, None, None)

    def pair(j, carry):
        stage(2 * j + 1, sb_ref, mb_ref, 2 * j, sa_ref, ma_ref)
        stage(2 * j + 2, sa_ref, ma_ref, 2 * j + 1, sb_ref, mb_ref)
        return carry

    lax.fori_loop(0, n_items // 2 - 1, pair, 0)
    stage(n_items - 1, sb_ref, mb_ref, n_items - 2, sa_ref, ma_ref)
    stage(None, None, None, n_items - 1, sb_ref, mb_ref)

    for qb in range(nqb):
        oT = jnp.concatenate([oT_ref[h, qb] for h in range(hb)], axis=0)
        o_ref[0, qb * tq:(qb + 1) * tq, :] = oT.T.astype(o_ref.dtype)


def _attention(qT, k, vT, *, tk=256, hb=2):
    B, H, nqb, kw, tq = qT.shape
    S = k.shape[2]
    dv = vT.shape[2]
    assert (hb * nqb) % 2 == 0 and hb * dv == V7X_LANES
    return pl.pallas_call(
        functools.partial(_attn_kernel, tk=tk),
        grid=(B, H // hb),
        in_specs=[
            pl.BlockSpec((1, hb, nqb, kw, tq), lambda b, h: (b, h, 0, 0, 0)),
            pl.BlockSpec((1, hb, S, kw), lambda b, h: (b, h, 0, 0)),
            pl.BlockSpec((1, hb, dv, S), lambda b, h: (b, h, 0, 0)),
        ],
        out_specs=pl.BlockSpec((1, S, hb * dv), lambda b, h: (b, 0, h)),
        out_shape=jax.ShapeDtypeStruct((B, S, H * dv), BF16),
        scratch_shapes=[
            pltpu.VMEM((S, tq), F32), pltpu.VMEM((S, tq), F32),
            pltpu.VMEM((8, tq), F32), pltpu.VMEM((8, tq), F32),
            pltpu.VMEM((hb, nqb, dv, tq), F32),
        ],
        compiler_params=_cparams(2),
        name="mla_attention",
    )(qT, k, vT)


def _proj_ln_kernel(x_ref, a_ref, w_ref, g_ref, b_ref, o_ref, *, alpha):
    h = jnp.dot(a_ref[...], w_ref[...], preferred_element_type=F32)
    o_ref[...] = _layer_norm(alpha * x_ref[...] + h, g_ref[...], b_ref[...])


def _proj_ln(x2, a2, w, g, b, alpha, *, tm=1024):
    T, D = x2.shape
    K = a2.shape[1]
    return pl.pallas_call(
        functools.partial(_proj_ln_kernel, alpha=alpha),
        grid=(T // tm,),
        in_specs=[
            pl.BlockSpec((tm, D), lambda i: (i, 0)),
            pl.BlockSpec((tm, K), lambda i: (i, 0)),
            _const_spec((K, D)), _const_spec((1, D)), _const_spec((1, D)),
        ],
        out_specs=pl.BlockSpec((tm, D), lambda i: (i, 0)),
        out_shape=jax.ShapeDtypeStruct((T, D), F32),
        compiler_params=_cparams(1),
        name="out_proj_ln",
    )(x2, a2, w.astype(BF16), g.reshape(1, D), b.reshape(1, D))


def _ffn_kernel(x_ref, w1_ref, w3_ref, w2_ref, g_ref, b_ref, o_ref, gate_ref, *, alpha, fc):
    x = x_ref[...]
    xb = x.astype(BF16)
    dff = w1_ref.shape[1]
    for c in range(dff // fc):
        a = jnp.dot(xb, w1_ref[:, c * fc:(c + 1) * fc], preferred_element_type=F32)
        u = jnp.dot(xb, w3_ref[:, c * fc:(c + 1) * fc], preferred_element_type=F32)
        gate_ref[:, c * fc:(c + 1) * fc] = (a * _sigmoid(a) * u).astype(BF16)
    f = jnp.dot(gate_ref[...], w2_ref[...], preferred_element_type=F32)
    o_ref[...] = _layer_norm(alpha * x + f, g_ref[...], b_ref[...])


def _ffn_ln(x2, w1, w3, w2, g, b, alpha, *, tm=1024, fc=256):
    T, D = x2.shape
    dff = w1.shape[1]
    return pl.pallas_call(
        functools.partial(_ffn_kernel, alpha=alpha, fc=fc),
        grid=(T // tm,),
        in_specs=[
            pl.BlockSpec((tm, D), lambda i: (i, 0)),
            _const_spec((D, dff)), _const_spec((D, dff)), _const_spec((dff, D)),
            _const_spec((1, D)), _const_spec((1, D)),
        ],
        out_specs=pl.BlockSpec((tm, D), lambda i: (i, 0)),
        out_shape=jax.ShapeDtypeStruct((T, D), F32),
        scratch_shapes=[pltpu.VMEM((tm, dff), BF16)],
        compiler_params=_cparams(1),
        name="ffn_ln",
    )(x2, w1.astype(BF16), w3.astype(BF16), w2.astype(BF16), g.reshape(1, D), b.reshape(1, D))


def _glu_kernel(x_ref, w_ref, b_ref, o_ref):
    D = o_ref.shape[1]
    h = jnp.dot(x_ref[...].astype(BF16), w_ref[...], preferred_element_type=F32) + b_ref[...]
    o_ref[...] = h[:, :D] * _sigmoid(h[:, D:])


def _conv_glu(x2, w_in, b_in, *, tm=1024):
    T, D = x2.shape
    return pl.pallas_call(
        _glu_kernel,
        grid=(T // tm,),
        in_specs=[pl.BlockSpec((tm, D), lambda i: (i, 0)), _const_spec((D, 2 * D)), _const_spec((1, 2 * D))],
        out_specs=pl.BlockSpec((tm, D), lambda i: (i, 0)),
        out_shape=jax.ShapeDtypeStruct((T, D), F32),
        compiler_params=_cparams(1),
        name="conv_glu",
    )(x2, w_in.astype(BF16), b_in.reshape(1, 2 * D))


def _conv_mid_kernel(x_ref, h_ref, hp_ref, hn_ref, dw_ref, dwb_ref, lg_ref, lb_ref, wo_ref, bo_ref,
                     g_ref, b_ref, o_ref, buf_ref, cv_ref, *, alpha, width, rb):
    ts = h_ref.shape[1]
    nl = buf_ref.shape[0]
    L = V7X_LANES
    i = pl.program_id(1)
    n = pl.num_programs(1)
    prev_ok = (i > 0).astype(F32)
    next_ok = (i < n - 1).astype(F32)
    for lc in range(nl):
        buf_ref[lc, 0:HALO, :] = hp_ref[0, :, lc * L:(lc + 1) * L] * prev_ok
        buf_ref[lc, HALO:HALO + ts, :] = h_ref[0, :, lc * L:(lc + 1) * L]
        buf_ref[lc, HALO + ts:, :] = hn_ref[0, :, lc * L:(lc + 1) * L] * next_ok

    pad = width // 2
    off = HALO - pad

    def lane_chunk(lc, carry):
        for r0 in range(0, ts, rb):
            acc = jnp.zeros((rb, L), F32)
            for j in range(width):
                acc = acc + buf_ref[lc, r0 + off + j:r0 + off + j + rb, :] * dw_ref[lc, j:j + 1, :]
            cv_ref[lc, r0:r0 + rb, :] = acc + dwb_ref[lc]
        return carry

    lax.fori_loop(0, nl, lane_chunk, 0)

    cv = jnp.concatenate([cv_ref[lc] for lc in range(nl)], axis=1)
    y = _layer_norm(cv, lg_ref[...], lb_ref[...])
    y = y * _sigmoid(y)
    hout = jnp.dot(y.astype(BF16), wo_ref[...], preferred_element_type=F32) + bo_ref[...]
    o_ref[0] = _layer_norm(alpha * x_ref[0] + hout, g_ref[...], b_ref[...])


def _conv_mid(x, h, dw, dw_b, ln_g, ln_b, w_out, b_out, g, b, alpha, *, ts=512, rb=256):
    B, S, D = x.shape
    width = dw.shape[0]
    L = V7X_LANES
    nl = D // L
    assert width // 2 <= HALO and ts % HALO == 0 and ts % rb == 0
    hblk = ts // HALO
    dw3 = jnp.pad(dw, ((0, 2 * HALO - width), (0, 0))).reshape(2 * HALO, nl, L).transpose(1, 0, 2)
    dwb3 = dw_b.reshape(nl, 1, L)
    row = lambda v: v.reshape(1, D)
    kern = functools.partial(_conv_mid_kernel, alpha=alpha, width=width, rb=rb)
    return pl.pallas_call(
        kern,
        grid=(B, S // ts),
        in_specs=[
            pl.BlockSpec((1, ts, D), lambda bb, i: (bb, i, 0)),
            pl.BlockSpec((1, ts, D), lambda bb, i: (bb, i, 0)),
            pl.BlockSpec((1, HALO, D), lambda bb, i: (bb, jnp.maximum(i * hblk - 1, 0), 0)),
            pl.BlockSpec((1, HALO, D), lambda bb, i: (bb, jnp.minimum((i + 1) * hblk, S // HALO - 1), 0)),
            _const_spec(dw3.shape), _const_spec(dwb3.shape),
            _const_spec((1, D)), _const_spec((1, D)),
            _const_spec((D, D)), _const_spec((1, D)),
            _const_spec((1, D)), _const_spec((1, D)),
        ],
        out_specs=pl.BlockSpec((1, ts, D), lambda bb, i: (bb, i, 0)),
        out_shape=jax.ShapeDtypeStruct((B, S, D), F32),
        scratch_shapes=[pltpu.VMEM((nl, ts + 2 * HALO, L), F32), pltpu.VMEM((nl, ts, L), F32)],
        compiler_params=_cparams(2),
        name="conv_mid",
    )(x, h, h, h, dw3, dwb3, row(ln_g), row(ln_b), w_out.astype(BF16), row(b_out), row(g), row(b))


def kernel(x, positions, ln_g, ln_b, mla_w_dq, mla_q_norm, mla_w_uq, mla_w_dkv, mla_kv_norm, mla_w_ukv, mla_w_o, conv_w_in, conv_b_in, conv_dw, conv_dw_b, conv_ln_g, conv_ln_b, conv_w_out, conv_b_out, ffn_w1, ffn_w3, ffn_w2):
    B, S, D = x.shape
    depth = ln_g.shape[0]
    alpha = float((2 * depth) ** 0.25)
    cos, sin = _rope_tables(positions)
    for i in range(depth):
        j = i // 2
        if i % 2 == 0:
            qT, k, vT = _mla_proj(x, cos, sin, mla_w_dq[j], mla_q_norm[j], mla_w_uq[j],
                                  mla_w_dkv[j], mla_kv_norm[j], mla_w_ukv[j], tq=512)
            o = _attention(qT, k, vT)
            x2 = _proj_ln(x.reshape(B * S, D), o.reshape(B * S, -1), mla_w_o[j],
                          ln_g[i, 0], ln_b[i, 0], alpha)
        else:
            h = _conv_glu(x.reshape(B * S, D), conv_w_in[j], conv_b_in[j]).reshape(B, S, D)
            x2 = _conv_mid(x, h, conv_dw[j], conv_dw_b[j], conv_ln_g[j], conv_ln_b[j],
                           conv_w_out[j], conv_b_out[j], ln_g[i, 0], ln_b[i, 0], alpha).reshape(B * S, D)
        x2 = _ffn_ln(x2, ffn_w1[i], ffn_w3[i], ffn_w2[i], ln_g[i, 1], ln_b[i, 1], alpha)
        x = x2.reshape(B, S, D)
    return x
```

```python
import functools
import math

import jax
import jax.numpy as jnp
from jax import lax
from jax.experimental import pallas as pl
from jax.experimental.pallas import tpu as pltpu

F32 = jnp.float32
BF16 = jnp.bfloat16

N_HEADS = 16
QK_NOPE_DIM = 64
QK_ROPE_DIM = 32
V_HEAD_DIM = 64
ROPE_THETA = 10000.0
LN_EPS = 1e-5
RMS_EPS = 1e-6

V7X_LANES = 128
V7X_VMEM_BYTES = 64 * 1024 * 1024
VMEM_LIMIT = (V7X_VMEM_BYTES * 3) // 4

HALO = 16


def _cparams(n_axes):
    return pltpu.CompilerParams(
        dimension_semantics=("arbitrary",) * n_axes, vmem_limit_bytes=VMEM_LIMIT)


def _const_spec(shape):
    nd = len(shape)
    return pl.BlockSpec(shape, lambda *_: (0,) * nd, pipeline_mode=pl.Buffered(1))


def _layer_norm(z, g, b):
    mu = jnp.mean(z, axis=-1, keepdims=True)
    d = z - mu
    var = jnp.mean(d * d, axis=-1, keepdims=True)
    return d * lax.rsqrt(var + LN_EPS) * g + b


def _sigmoid(a):
    return 1.0 / (1.0 + jnp.exp(-a))


def _rope_kernel(pos_ref, inv_ref, cos_ref, sin_ref):
    ang = inv_ref[...] * pos_ref[0].astype(F32)
    cos_ref[0] = jnp.cos(ang)
    sin_ref[0] = jnp.sin(ang)


def _rope_tables(positions):
    B, S = positions.shape
    hr = QK_ROPE_DIM // 2
    inv_freq = ROPE_THETA ** (-jnp.arange(0, QK_ROPE_DIM, 2, dtype=F32) / QK_ROPE_DIM)
    return pl.pallas_call(
        _rope_kernel,
        grid=(B,),
        in_specs=[pl.BlockSpec((1, 1, S), lambda b: (b, 0, 0)), _const_spec((hr, 1))],
        out_specs=[pl.BlockSpec((1, hr, S), lambda b: (b, 0, 0))] * 2,
        out_shape=[jax.ShapeDtypeStruct((B, hr, S), F32)] * 2,
        compiler_params=_cparams(1),
        name="rope_tables",
    )(positions.reshape(B, 1, S), inv_freq.reshape(hr, 1))


def _mla_proj_kernel(x_ref, cos_ref, sin_ref, wcat_ref, qn_ref, kvn_ref, wuq_ref, wuv_ref, wuk_ref,
                     qT_ref, k_ref, vT_ref, *, ql, kvl, qscale):
    H, dn, dr, dv = N_HEADS, QK_NOPE_DIM, QK_ROPE_DIM, V_HEAD_DIM
    hr = dr // 2
    tp = x_ref.shape[1]
    xb = x_ref[0].astype(BF16)
    zT = lax.dot_general(wcat_ref[...], xb, (((1,), (1,)), ((), ())),
                         preferred_element_type=F32)
    cqT = zT[0:ql]
    ckvT = zT[ql:ql + kvl]
    krT = zT[ql + kvl:ql + kvl + dr]
    cqT = cqT * lax.rsqrt(jnp.mean(cqT * cqT, axis=0, keepdims=True) + RMS_EPS) * qn_ref[...]
    ckvT = ckvT * lax.rsqrt(jnp.mean(ckvT * ckvT, axis=0, keepdims=True) + RMS_EPS) * kvn_ref[...]
    cos = cos_ref[0]
    sin = sin_ref[0]

    qT = jnp.dot(wuq_ref[...], cqT.astype(BF16), preferred_element_type=F32)
    qT = qT.reshape(H, dn + dr, tp)
    t1 = qT[:, dn:dn + hr]
    t2 = qT[:, dn + hr:dn + dr]
    r1 = (t1 * cos - t2 * sin) * qscale
    r2 = (t2 * cos + t1 * sin) * qscale
    qn = qT[:, 0:dn] * qscale
    nqb, kwq, tq = qT_ref.shape[2:]
    for j in range(nqb):
        cols = slice(j * tq, (j + 1) * tq)
        qT_ref[0, :, j, 0:dn] = qn[:, :, cols].astype(BF16)
        qT_ref[0, :, j, dn:dn + hr] = r1[:, :, cols].astype(BF16)
        qT_ref[0, :, j, dn + hr:dn + dr] = r2[:, :, cols].astype(BF16)
        qT_ref[0, :, j, dn + dr:] = jnp.zeros((H, kwq - dn - dr, tq), BF16)

    ckvb = ckvT.astype(BF16)
    vT = jnp.dot(wuv_ref[...], ckvb, preferred_element_type=F32)
    vT_ref[0] = vT.reshape(H, dv, tp).astype(BF16)

    kw = k_ref.shape[3]
    kpad = jnp.dot(ckvT.T.astype(BF16), wuk_ref[...], preferred_element_type=F32)
    k1 = krT[0:hr]
    k2 = krT[hr:dr]
    krT_pad = jnp.concatenate(
        [jnp.zeros((dn, tp), F32), k1 * cos - k2 * sin, k2 * cos + k1 * sin,
         jnp.zeros((kw - dn - dr, tp), F32)], axis=0)
    kr_pad = krT_pad.T
    for h in range(H):
        k_ref[0, h] = (kpad[:, h * kw:(h + 1) * kw] + kr_pad).astype(BF16)


def _mla_proj(x, cos, sin, w_dq, q_norm, w_uq, w_dkv, kv_norm, w_ukv, *, tq, tp=1024):
    B, S, D = x.shape
    H, dn, dr, dv = N_HEADS, QK_NOPE_DIM, QK_ROPE_DIM, V_HEAD_DIM
    hr = dr // 2
    ql = w_dq.shape[1]
    kvl = kv_norm.shape[0]
    kw = V7X_LANES
    wcatT = jnp.concatenate([w_dq, w_dkv], axis=1).T.astype(BF16)
    wuqT = w_uq.T.astype(BF16)
    w_ukv3 = w_ukv.reshape(kvl, H, dn + dv)
    wuvT = w_ukv3[:, :, dn:].reshape(kvl, H * dv).T.astype(BF16)
    wukp = jnp.pad(w_ukv3[:, :, :dn], ((0, 0), (0, 0), (0, kw - dn))).reshape(kvl, H * kw).astype(BF16)
    qscale = math.log2(math.e) / math.sqrt(dn + dr)
    kern = functools.partial(_mla_proj_kernel, ql=ql, kvl=kvl, qscale=qscale)
    return pl.pallas_call(
        kern,
        grid=(B, S // tp),
        in_specs=[
            pl.BlockSpec((1, tp, D), lambda b, i: (b, i, 0)),
            pl.BlockSpec((1, hr, tp), lambda b, i: (b, 0, i)),
            pl.BlockSpec((1, hr, tp), lambda b, i: (b, 0, i)),
            _const_spec(wcatT.shape), _const_spec((ql, 1)), _const_spec((kvl, 1)),
            _const_spec(wuqT.shape), _const_spec(wuvT.shape), _const_spec(wukp.shape),
        ],
        out_specs=[
            pl.BlockSpec((1, H, tp // tq, kw, tq), lambda b, i: (b, 0, i, 0, 0)),
            pl.BlockSpec((1, H, tp, kw), lambda b, i: (b, 0, i, 0)),
            pl.BlockSpec((1, H, dv, tp), lambda b, i: (b, 0, 0, i)),
        ],
        out_shape=[
            jax.ShapeDtypeStruct((B, H, S // tq, kw, tq), BF16),
            jax.ShapeDtypeStruct((B, H, S, kw), BF16),
            jax.ShapeDtypeStruct((B, H, dv, S), BF16),
        ],
        compiler_params=_cparams(2),
        name="mla_proj",
    )(x, cos, sin, wcatT, q_norm.reshape(ql, 1), kv_norm.reshape(kvl, 1), wuqT, wuvT, wukp)


def _attn_kernel(qT_ref, k_ref, vT_ref, o_ref, sa_ref, sb_ref, ma_ref, mb_ref, oT_ref, *, tk):
    hb, nqb, _, tq = qT_ref.shape[1:]
    S = k_ref.shape[2]
    dv = vT_ref.shape[2]
    nc = S // tk
    n_items = hb * nqb

    def stage(t1, s1_ref, m1_ref, t2, s2_ref, m2_ref):
        if t1 is not None:
            h1 = t1 // nqb
            qT = qT_ref[0, h1, t1 % nqb]
            mx = jnp.full((8, tq), -jnp.inf, F32)
        if t2 is not None:
            h2 = t2 // nqb
            m8 = jnp.broadcast_to(jnp.max(m2_ref[...], axis=0, keepdims=True), (8, tq))
            l8 = jnp.zeros((8, tq), F32)
            acc = jnp.zeros((dv, tq), F32)
        for c in range(nc):
            rows = slice(c * tk, (c + 1) * tk)
            if t1 is not None:
                s = jnp.dot(k_ref[0, h1, rows, :], qT, preferred_element_type=F32)
                s1_ref[rows, :] = s
                mx = jnp.maximum(mx, jnp.max(s.reshape(tk // 8, 8, tq), axis=0))
            if t2 is not None:
                p = jnp.exp2(s2_ref[rows, :].reshape(tk // 8, 8, tq) - m8)
                l8 = l8 + jnp.sum(p, axis=0)
                acc = acc + jnp.dot(vT_ref[0, h2, :, rows], p.reshape(tk, tq).astype(BF16),
                                    preferred_element_type=F32)
        if t1 is not None:
            m1_ref[...] = mx
        if t2 is not None:
            oT_ref[h2, t2 % nqb] = acc / jnp.sum(l8, axis=0, keepdims=True)

    stage(0, sa_ref, ma_ref, None, None, None)

    def pair(j, carry):
        stage(2 * j + 1, sb_ref, mb_ref, 2 * j, sa_ref, ma_ref)
        stage(2 * j + 2, sa_ref, ma_ref, 2 * j + 1, sb_ref, mb_ref)
        return carry

    lax.fori_loop(0, n_items // 2 - 1, pair, 0)
    stage(n_items - 1, sb_ref, mb_ref, n_items - 2, sa_ref, ma_ref)
    stage(None, None, None, n_items - 1, sb_ref, mb_ref)

    for qb in range(nqb):
        oT = jnp.concatenate([oT_ref[h, qb] for h in range(hb)], axis=0)
        o_ref[0, qb * tq:(qb + 1) * tq, :] = oT.T.astype(o_ref.dtype)


def _attention(qT, k, vT, *, tk=256, hb=2):
    B, H, nqb, kw, tq = qT.shape
    S = k.shape[2]
    dv = vT.shape[2]
    assert (hb * nqb) % 2 == 0 and hb * dv == V7X_LANES
    return pl.pallas_call(
        functools.partial(_attn_kernel, tk=tk),
        grid=(B, H // hb),
        in_specs=[
            pl.BlockSpec((1, hb, nqb, kw, tq), lambda b, h: (b, h, 0, 0, 0)),
            pl.BlockSpec((1, hb, S, kw), lambda b, h: (b, h, 0, 0)),
            pl.BlockSpec((1, hb, dv, S), lambda b, h: (b, h, 0, 0)),
        ],
        out_specs=pl.BlockSpec((1, S, hb * dv), lambda b, h: (b, 0, h)),
        out_shape=jax.ShapeDtypeStruct((B, S, H * dv), BF16),
        scratch_shapes=[
            pltpu.VMEM((S, tq), F32), pltpu.VMEM((S, tq), F32),
            pltpu.VMEM((8, tq), F32), pltpu.VMEM((8, tq), F32),
            pltpu.VMEM((hb, nqb, dv, tq), F32),
        ],
        compiler_params=_cparams(2),
        name="mla_attention",
    )(qT, k, vT)


def _proj_ln_kernel(x_ref, a_ref, w_ref, g_ref, b_ref, o_ref, *, alpha):
    h = jnp.dot(a_ref[...], w_ref[...], preferred_element_type=F32)
    o_ref[...] = _layer_norm(alpha * x_ref[...] + h, g_ref[...], b_ref[...])


def _proj_ln(x2, a2, w, g, b, alpha, *, tm=1024):
    T, D = x2.shape
    K = a2.shape[1]
    return pl.pallas_call(
        functools.partial(_proj_ln_kernel, alpha=alpha),
        grid=(T // tm,),
        in_specs=[
            pl.BlockSpec((tm, D), lambda i: (i, 0)),
            pl.BlockSpec((tm, K), lambda i: (i, 0)),
            _const_spec((K, D)), _const_spec((1, D)), _const_spec((1, D)),
        ],
        out_specs=pl.BlockSpec((tm, D), lambda i: (i, 0)),
        out_shape=jax.ShapeDtypeStruct((T, D), F32),
        compiler_params=_cparams(1),
        name="out_proj_ln",
    )(x2, a2, w.astype(BF16), g.reshape(1, D), b.reshape(1, D))


def _ffn_kernel(x_ref, w1_ref, w3_ref, w2_ref, g_ref, b_ref, o_ref, gate_ref, *, alpha, fc):
    x = x_ref[...]
    xb = x.astype(BF16)
    dff = w1_ref.shape[1]
    for c in range(dff // fc):
        a = jnp.dot(xb, w1_ref[:, c * fc:(c + 1) * fc], preferred_element_type=F32)
        u = jnp.dot(xb, w3_ref[:, c * fc:(c + 1) * fc], preferred_element_type=F32)
        gate_ref[:, c * fc:(c + 1) * fc] = (a * _sigmoid(a) * u).astype(BF16)
    f = jnp.dot(gate_ref[...], w2_ref[...], preferred_element_type=F32)
    o_ref[...] = _layer_norm(alpha * x + f, g_ref[...], b_ref[...])


def _ffn_ln(x2, w1, w3, w2, g, b, alpha, *, tm=1024, fc=256):
    T, D = x2.shape
    dff = w1.shape[1]
    return pl.pallas_call(
        functools.partial(_ffn_kernel, alpha=alpha, fc=fc),
        grid=(T // tm,),
        in_specs=[
            pl.BlockSpec((tm, D), lambda i: (i, 0)),
            _const_spec((D, dff)), _const_spec((D, dff)), _const_spec((dff, D)),
            _const_spec((1, D)), _const_spec((1, D)),
        ],
        out_specs=pl.BlockSpec((tm, D), lambda i: (i, 0)),
        out_shape=jax.ShapeDtypeStruct((T, D), F32),
        scratch_shapes=[pltpu.VMEM((tm, dff), BF16)],
        compiler_params=_cparams(1),
        name="ffn_ln",
    )(x2, w1.astype(BF16), w3.astype(BF16), w2.astype(BF16), g.reshape(1, D), b.reshape(1, D))


def _glu_kernel(x_ref, w_ref, b_ref, o_ref):
    D = o_ref.shape[1]
    h = jnp.dot(x_ref[...].astype(BF16), w_ref[...], preferred_element_type=F32) + b_ref[...]
    o_ref[...] = h[:, :D] * _sigmoid(h[:, D:])


def _conv_glu(x2, w_in, b_in, *, tm=1024):
    T, D = x2.shape
    return pl.pallas_call(
        _glu_kernel,
        grid=(T // tm,),
        in_specs=[pl.BlockSpec((tm, D), lambda i: (i, 0)), _const_spec((D, 2 * D)), _const_spec((1, 2 * D))],
        out_specs=pl.BlockSpec((tm, D), lambda i: (i, 0)),
        out_shape=jax.ShapeDtypeStruct((T, D), F32),
        compiler_params=_cparams(1),
        name="conv_glu",
    )(x2, w_in.astype(BF16), b_in.reshape(1, 2 * D))


def _conv_mid_kernel(x_ref, h_ref, hp_ref, hn_ref, dw_ref, dwb_ref, lg_ref, lb_ref, wo_ref, bo_ref,
                     g_ref, b_ref, o_ref, buf_ref, cv_ref, *, alpha, width, rb):
    ts = h_ref.shape[1]
    nl = buf_ref.shape[0]
    L = V7X_LANES
    i = pl.program_id(1)
    n = pl.num_programs(1)
    prev_ok = (i > 0).astype(F32)
    next_ok = (i < n - 1).astype(F32)
    for lc in range(nl):
        buf_ref[lc, 0:HALO, :] = hp_ref[0, :, lc * L:(lc + 1) * L] * prev_ok
        buf_ref[lc, HALO:HALO + ts, :] = h_ref[0, :, lc * L:(lc + 1) * L]
        buf_ref[lc, HALO + ts:, :] = hn_ref[0, :, lc * L:(lc + 1) * L] * next_ok

    pad = width // 2
    off = HALO - pad

    def lane_chunk(lc, carry):
        for r0 in range(0, ts, rb):
            acc = jnp.zeros((rb, L), F32)
            for j in range(width):
                acc = acc + buf_ref[lc, r0 + off + j:r0 + off + j + rb, :] * dw_ref[lc, j:j + 1, :]
            cv_ref[lc, r0:r0 + rb, :] = acc + dwb_ref[lc]
        return carry

    lax.fori_loop(0, nl, lane_chunk, 0)

    cv = jnp.concatenate([cv_ref[lc] for lc in range(nl)], axis=1)
    y = _layer_norm(cv, lg_ref[...], lb_ref[...])
    y = y * _sigmoid(y)
    hout = jnp.dot(y.astype(BF16), wo_ref[...], preferred_element_type=F32) + bo_ref[...]
    o_ref[0] = _layer_norm(alpha * x_ref[0] + hout, g_ref[...], b_ref[...])


def _conv_mid(x, h, dw, dw_b, ln_g, ln_b, w_out, b_out, g, b, alpha, *, ts=1024, rb=256):
    B, S, D = x.shape
    width = dw.shape[0]
    L = V7X_LANES
    nl = D // L
    assert width // 2 <= HALO and ts % HALO == 0 and ts % rb == 0
    hblk = ts // HALO
    dw3 = jnp.pad(dw, ((0, 2 * HALO - width), (0, 0))).reshape(2 * HALO, nl, L).transpose(1, 0, 2)
    dwb3 = dw_b.reshape(nl, 1, L)
    row = lambda v: v.reshape(1, D)
    kern = functools.partial(_conv_mid_kernel, alpha=alpha, width=width, rb=rb)
    return pl.pallas_call(
        kern,
        grid=(B, S // ts),
        in_specs=[
            pl.BlockSpec((1, ts, D), lambda bb, i: (bb, i, 0)),
            pl.BlockSpec((1, ts, D), lambda bb, i: (bb, i, 0)),
            pl.BlockSpec((1, HALO, D), lambda bb, i: (bb, jnp.maximum(i * hblk - 1, 0), 0)),
            pl.BlockSpec((1, HALO, D), lambda bb, i: (bb, jnp.minimum((i + 1) * hblk, S // HALO - 1), 0)),
            _const_spec(dw3.shape), _const_spec(dwb3.shape),
            _const_spec((1, D)), _const_spec((1, D)),
            _const_spec((D, D)), _const_spec((1, D)),
            _const_spec((1, D)), _const_spec((1, D)),
        ],
        out_specs=pl.BlockSpec((1, ts, D), lambda bb, i: (bb, i, 0)),
        out_shape=jax.ShapeDtypeStruct((B, S, D), F32),
        scratch_shapes=[pltpu.VMEM((nl, ts + 2 * HALO, L), F32), pltpu.VMEM((nl, ts, L), F32)],
        compiler_params=_cparams(2),
        name="conv_mid",
    )(x, h, h, h, dw3, dwb3, row(ln_g), row(ln_b), w_out.astype(BF16), row(b_out), row(g), row(b))


def kernel(x, positions, ln_g, ln_b, mla_w_dq, mla_q_norm, mla_w_uq, mla_w_dkv, mla_kv_norm, mla_w_ukv, mla_w_o, conv_w_in, conv_b_in, conv_dw, conv_dw_b, conv_ln_g, conv_ln_b, conv_w_out, conv_b_out, ffn_w1, ffn_w3, ffn_w2):
    B, S, D = x.shape
    depth = ln_g.shape[0]
    alpha = float((2 * depth) ** 0.25)
    cos, sin = _rope_tables(positions)
    for i in range(depth):
        j = i // 2
        if i % 2 == 0:
            qT, k, vT = _mla_proj(x, cos, sin, mla_w_dq[j], mla_q_norm[j], mla_w_uq[j],
                                  mla_w_dkv[j], mla_kv_norm[j], mla_w_ukv[j], tq=512)
            o = _attention(qT, k, vT)
            x2 = _proj_ln(x.reshape(B * S, D), o.reshape(B * S, -1), mla_w_o[j],
                          ln_g[i, 0], ln_b[i, 0], alpha)
        else:
            h = _conv_glu(x.reshape(B * S, D), conv_w_in[j], conv_b_in[j]).reshape(B, S, D)
            x2 = _conv_mid(x, h, conv_dw[j], conv_dw_b[j], conv_ln_g[j], conv_ln_b[j],
                           conv_w_out[j], conv_b_out[j], ln_g[i, 0], ln_b[i, 0], alpha).reshape(B * S, D)
        x2 = _ffn_ln(x2, ffn_w1[i], ffn_w3[i], ffn_w2[i], ln_g[i, 1], ln_b[i, 1], alpha)
        x = x2.reshape(B, S, D)
    return x
```
